```python
import math
import jax, jax.numpy as jnp
from jax import lax
import numpy as np

D_MODEL = 1024
BATCH = 8
SEQ = 4096
DEPTH = 2
DEC_BATCH = 8
DEC_SEQ = 2048
PAST_LEN = 128

POOL_WINDOWS = (2, 4, 8, 16)
N_POOL_GROUPS = 4
POOL_GROUP_DIM = D_MODEL // 8
POOL_WIDTH = N_POOL_GROUPS * POOL_GROUP_DIM
SGU_HEADS = 4
SGU_HEAD_DIM = D_MODEL // 8
SGU_WIDTH = SGU_HEADS * SGU_HEAD_DIM
CHUNK = 128
IN_WIDTH = POOL_WIDTH + 2 * SGU_WIDTH
MIX_WIDTH = POOL_WIDTH + SGU_WIDTH
CONV_WIDTH = D_MODEL
CONV_TAPS = 31
D_FF = 4 * D_MODEL
N_MOD = 6
ALPHA = (2.0 * DEPTH) ** 0.25
BETA = (8.0 * DEPTH) ** -0.25
LN_EPS = 1e-5

kernel_name = "hybrid_pool_sgu_conformer_encoder"


def layer_norm(x, g, b):
    xf = x.astype(jnp.float32)
    mu = jnp.mean(xf, axis=-1, keepdims=True)
    var = jnp.mean(jnp.square(xf - mu), axis=-1, keepdims=True)
    return ((xf - mu) * lax.rsqrt(var + LN_EPS) * g.astype(jnp.float32) + b.astype(jnp.float32)).astype(x.dtype)


def adaln(c, w, b):
    mod = jax.nn.silu(c) @ w + b
    return jnp.split(mod, N_MOD, axis=-1)


def modulate(x, shift, scale):
    return x * (1.0 + scale[:, None, :]) + shift[:, None, :]


def post_norm_residual(x, y, gate, g, b):
    return layer_norm(ALPHA * x + gate[:, None, :] * y, g, b)


def centred_mean_minus_identity(a, window):
    s = a.shape[1]
    af = a.astype(jnp.float32)
    cs = jnp.concatenate([jnp.zeros_like(af[:, :1]), jnp.cumsum(af, axis=1)], axis=1)
    t = jnp.arange(s)
    lo = jnp.clip(t - window // 2, 0, s)
    hi = jnp.clip(t + window // 2, 0, s)
    total = jnp.take(cs, hi, axis=1) - jnp.take(cs, lo, axis=1)
    cnt = (hi - lo).astype(jnp.float32)[None, :, None]
    return (total / cnt - af).astype(a.dtype)


def pool_mixer(a, pool_w, pool_scale):
    b, s, _ = a.shape
    groups = jnp.split(a, N_POOL_GROUPS, axis=-1)
    pooled = jnp.stack([centred_mean_minus_identity(gx, w) for gx, w in zip(groups, POOL_WINDOWS)], axis=2)
    mixed = jnp.einsum('bsgd,gde->bsge', pooled, pool_w).reshape(b, s, POOL_WIDTH)
    return mixed * pool_scale


def spatial_gating(u, v, ln_g, ln_b, sgu_w, sgu_b):
    b, s, _ = u.shape
    v = layer_norm(v, ln_g, ln_b)
    vc = v.reshape(b, s // CHUNK, CHUNK, SGU_HEADS, SGU_HEAD_DIM)
    vm = jnp.einsum('hpq,bnqhd->bnphd', sgu_w, vc) + sgu_b.T[:, :, None]
    return u * vm.reshape(b, s, SGU_WIDTH)


def channel_mixer(x, shift, scale, gate, w1, w2, g, b):
    h = modulate(x, shift, scale)
    y = jnp.square(jax.nn.relu(h @ w1)) @ w2
    return post_norm_residual(x, y, gate, g, b)


def even_layer(x, c, ada_w, ada_b, in_w, pool_w, pool_scale, sgu_ln_g, sgu_ln_b, sgu_w, sgu_b,
               out_w, ln1_g, ln1_b, mlp_w1, mlp_w2, ln2_g, ln2_b):
    sh_m, sc_m, gt_m, sh_f, sc_f, gt_f = adaln(c, ada_w, ada_b)
    h = modulate(x, sh_m, sc_m)
    z = h @ in_w
    a = z[..., :POOL_WIDTH]
    uv = jax.nn.gelu(z[..., POOL_WIDTH:], approximate=False)
    u, v = uv[..., :SGU_WIDTH], uv[..., SGU_WIDTH:]
    y_a = pool_mixer(a, pool_w, pool_scale)
    y_b = spatial_gating(u, v, sgu_ln_g, sgu_ln_b, sgu_w, sgu_b)
    y = jnp.concatenate([y_a, y_b], axis=-1) @ out_w
    x = post_norm_residual(x, y, gt_m, ln1_g, ln1_b)
    return channel_mixer(x, sh_f, sc_f, gt_f, mlp_w1, mlp_w2, ln2_g, ln2_b)


def odd_layer(x, c, ada_w, ada_b, pw1_w, pw1_b, dw_w, dw_b, cnorm_g, cnorm_b, pw2_w, pw2_b,
              ln1_g, ln1_b, mlp_w1, mlp_w2, ln2_g, ln2_b):
    sh_m, sc_m, gt_m, sh_f, sc_f, gt_f = adaln(c, ada_w, ada_b)
    h = modulate(x, sh_m, sc_m)
    p = h @ pw1_w + pw1_b
    g = p[..., :CONV_WIDTH] * jax.nn.sigmoid(p[..., CONV_WIDTH:])
    pad = CONV_TAPS // 2
    d = lax.conv_general_dilated(g, dw_w[:, None, :].astype(g.dtype), window_strides=(1,),
                                 padding=[(pad, pad)], dimension_numbers=('NWC', 'WIO', 'NWC'),
                                 feature_group_count=CONV_WIDTH) + dw_b
    d = jax.nn.silu(layer_norm(d, cnorm_g, cnorm_b))
    y = d @ pw2_w + pw2_b
    x = post_norm_residual(x, y, gt_m, ln1_g, ln1_b)
    return channel_mixer(x, sh_f, sc_f, gt_f, mlp_w1, mlp_w2, ln2_g, ln2_b)


def trunk(x, c, even_params, odd_params):
    for i in range(DEPTH):
        if i % 2 == 0:
            x = even_layer(x, c, *even_params)
        else:
            x = odd_layer(x, c, *odd_params)
    return x


def setup_inputs(seed: int = 0) -> dict:
    key = jax.random.key(seed)
    ks = iter(jax.random.split(key, 48))

    def nrm(shape, scale):
        return jax.random.normal(next(ks), shape, jnp.float32) * scale

    def gain(n):
        return 1.0 + nrm((n,), 0.02)

    def bias(shape):
        return nrm(shape, 0.02)

    d = D_MODEL
    return {
        "x_prompt": nrm((BATCH, SEQ, d), 1.0),
        "x_sample": nrm((DEC_BATCH, DEC_SEQ, d), 1.0),
        "c_prompt": nrm((BATCH, d), 1.0),
        "c_sample": nrm((DEC_BATCH, d), 1.0),
        "l0_ada_w": nrm((d, N_MOD * d), 0.5 * d ** -0.5),
        "l0_ada_b": bias((N_MOD * d,)),
        "l0_in_w": nrm((d, IN_WIDTH), d ** -0.5),
        "l0_pool_w": nrm((N_POOL_GROUPS, POOL_GROUP_DIM, POOL_GROUP_DIM), POOL_GROUP_DIM ** -0.5),
        "l0_pool_scale": gain(POOL_WIDTH),
        "l0_sgu_ln_g": gain(SGU_WIDTH),
        "l0_sgu_ln_b": bias((SGU_WIDTH,)),
        "l0_sgu_w": nrm((SGU_HEADS, CHUNK, CHUNK), CHUNK ** -0.5),
        "l0_sgu_b": 1.0 + nrm((SGU_HEADS, CHUNK), 0.02),
        "l0_out_w": nrm((MIX_WIDTH, d), BETA * MIX_WIDTH ** -0.5),
        "l0_ln1_g": gain(d),
        "l0_ln1_b": bias((d,)),
        "l0_mlp_w1": nrm((d, D_FF), d ** -0.5),
        "l0_mlp_w2": nrm((D_FF, d), BETA * D_FF ** -0.5),
        "l0_ln2_g": gain(d),
        "l0_ln2_b": bias((d,)),
        "l1_ada_w": nrm((d, N_MOD * d), 0.5 * d ** -0.5),
        "l1_ada_b": bias((N_MOD * d,)),
        "l1_pw1_w": nrm((d, 2 * CONV_WIDTH), d ** -0.5),
        "l1_pw1_b": bias((2 * CONV_WIDTH,)),
        "l1_dw_w": nrm((CONV_TAPS, CONV_WIDTH), CONV_TAPS ** -0.5),
        "l1_dw_b": bias((CONV_WIDTH,)),
        "l1_cnorm_g": gain(CONV_WIDTH),
        "l1_cnorm_b": bias((CONV_WIDTH,)),
        "l1_pw2_w": nrm((CONV_WIDTH, d), BETA * CONV_WIDTH ** -0.5),
        "l1_pw2_b": bias((d,)),
        "l1_ln1_g": gain(d),
        "l1_ln1_b": bias((d,)),
        "l1_mlp_w1": nrm((d, D_FF), d ** -0.5),
        "l1_mlp_w2": nrm((D_FF, d), BETA * D_FF ** -0.5),
        "l1_ln2_g": gain(d),
        "l1_ln2_b": bias((d,)),
    }


def reference(x_prompt, x_sample, c_prompt, c_sample,
              l0_ada_w, l0_ada_b, l0_in_w, l0_pool_w, l0_pool_scale, l0_sgu_ln_g, l0_sgu_ln_b,
              l0_sgu_w, l0_sgu_b, l0_out_w, l0_ln1_g, l0_ln1_b, l0_mlp_w1, l0_mlp_w2, l0_ln2_g, l0_ln2_b,
              l1_ada_w, l1_ada_b, l1_pw1_w, l1_pw1_b, l1_dw_w, l1_dw_b, l1_cnorm_g, l1_cnorm_b,
              l1_pw2_w, l1_pw2_b, l1_ln1_g, l1_ln1_b, l1_mlp_w1, l1_mlp_w2, l1_ln2_g, l1_ln2_b):
    even_params = (l0_ada_w, l0_ada_b, l0_in_w, l0_pool_w, l0_pool_scale, l0_sgu_ln_g, l0_sgu_ln_b,
                   l0_sgu_w, l0_sgu_b, l0_out_w, l0_ln1_g, l0_ln1_b, l0_mlp_w1, l0_mlp_w2, l0_ln2_g, l0_ln2_b)
    odd_params = (l1_ada_w, l1_ada_b, l1_pw1_w, l1_pw1_b, l1_dw_w, l1_dw_b, l1_cnorm_g, l1_cnorm_b,
                  l1_pw2_w, l1_pw2_b, l1_ln1_g, l1_ln1_b, l1_mlp_w1, l1_mlp_w2, l1_ln2_g, l1_ln2_b)
    y_prompt = trunk(x_prompt, c_prompt, even_params, odd_params)
    y_sample = trunk(x_sample, c_sample, even_params, odd_params)
    return (y_prompt, y_sample)
```

```python
import functools

import jax
import jax.numpy as jnp
from jax import lax
from jax.experimental import pallas as pl
from jax.experimental.pallas import tpu as pltpu

D_MODEL = 1024
DEPTH = 2
POOL_WINDOWS = (2, 4, 8, 16)
N_POOL_GROUPS = 4
POOL_GROUP_DIM = 128
POOL_WIDTH = 512
SGU_HEADS = 4
SGU_HEAD_DIM = 128
SGU_WIDTH = 512
CHUNK = 128
IN_WIDTH = POOL_WIDTH + 2 * SGU_WIDTH
CONV_WIDTH = D_MODEL
CONV_TAPS = 31
D_FF = 4 * D_MODEL
N_MOD = 6
ALPHA = (2.0 * DEPTH) ** 0.25
LN_EPS = 1e-5

F32 = jnp.float32
BF16 = jnp.bfloat16

V7X_VMEM_BYTES = 64 * 1024 * 1024
VMEM_LIMIT_BYTES = V7X_VMEM_BYTES - 8 * 1024 * 1024

SEQ_TILE = 512
HALO = 16
FF_CHUNK = 1024
CONV_ROWS = 64
CONV_COLS = 256
ADA_COLS = 1536


def _layer_norm(r, g, b):
    mu = jnp.mean(r, axis=-1, keepdims=True)
    d = r - mu
    var = jnp.mean(d * d, axis=-1, keepdims=True)
    return d * lax.rsqrt(var + LN_EPS) * g + b


def _sigmoid(x):
    return 1.0 / (1.0 + jnp.exp(-x))


def _bdot(a, b):
    return jnp.dot(a, b, preferred_element_type=F32)


def _ada_body(c_ref, w_ref, b_ref, o_ref):
    c = c_ref[...]
    s = (c * _sigmoid(c)).astype(BF16)
    o_ref[...] = _bdot(s, w_ref[...].astype(BF16)) + b_ref[...]


def _adaln(c, w, b):
    m, d = c.shape
    n = w.shape[1]
    return pl.pallas_call(
        _ada_body,
        grid=(n // ADA_COLS,),
        in_specs=[
            pl.BlockSpec((m, d), lambda j: (0, 0)),
            pl.BlockSpec((d, ADA_COLS), lambda j: (0, j)),
            pl.BlockSpec((1, ADA_COLS), lambda j: (0, j)),
        ],
        out_specs=pl.BlockSpec((m, ADA_COLS), lambda j: (0, j)),
        out_shape=jax.ShapeDtypeStruct((m, n), F32),
        compiler_params=pltpu.CompilerParams(vmem_limit_bytes=VMEM_LIMIT_BYTES),
        name="adaln",
    )(c, w, b.reshape(1, n))


def _const_spec(shape):
    nd = len(shape)
    return pl.BlockSpec(shape, lambda b, i: (0,) * nd)


def _mod_spec(row_off):
    return pl.BlockSpec((1, 1, D_MODEL), lambda b, i: (b + row_off, 0, 0))


def _x_specs(seq):
    per_tile = SEQ_TILE // HALO
    last = seq // HALO - 1
    prev = pl.BlockSpec((1, HALO, D_MODEL), lambda b, i: (b, jnp.maximum(i * per_tile - 1, 0), 0))
    main = pl.BlockSpec((1, SEQ_TILE, D_MODEL), lambda b, i: (b, i, 0))
    nxt = pl.BlockSpec((1, HALO, D_MODEL), lambda b, i: (b, jnp.minimum((i + 1) * per_tile, last), 0))
    return prev, main, nxt


def _modulated_ext(xp_ref, x_ref, xn_ref, sh_ref, sc_ref):
    x_ext = jnp.concatenate([xp_ref[0], x_ref[0], xn_ref[0]], axis=0)
    return (x_ext * (1.0 + sc_ref[0]) + sh_ref[0]).astype(BF16)


def _row_valid(seq):
    i = pl.program_id(1)
    pos = i * SEQ_TILE - HALO + lax.broadcasted_iota(jnp.int32, (SEQ_TILE + 2 * HALO, 1), 0)
    return (pos >= 0) & (pos < seq)


def _mix0_body(seq, xp_ref, x_ref, xn_ref, sh_ref, sc_ref, gt_ref, inw_ref, poolw_ref, pscale_ref,
               sg_ref, sb_ref, sguw_ref, sbias_ref, outw_ref, g_ref, b_ref, o_ref, a_s):
    h_ext = _modulated_ext(xp_ref, x_ref, xn_ref, sh_ref, sc_ref)
    z = _bdot(h_ext, inw_ref[...])
    a_s[...] = jnp.where(_row_valid(seq), z[:, :POOL_WIDTH], 0.0)

    pos = pl.program_id(1) * SEQ_TILE + lax.broadcasted_iota(jnp.int32, (SEQ_TILE, 1), 0)
    pooled = []
    for grp, window in enumerate(POOL_WINDOWS):
        half = window // 2
        lanes = slice(grp * POOL_GROUP_DIM, (grp + 1) * POOL_GROUP_DIM)
        total = a_s[pl.ds(HALO - half, SEQ_TILE), lanes]
        for k in range(-half + 1, half):
            total = total + a_s[pl.ds(HALO + k, SEQ_TILE), lanes]
        cnt = jnp.minimum(pos + half, seq) - jnp.maximum(pos - half, 0)
        pooled.append(total * (1.0 / cnt.astype(F32)) - a_s[pl.ds(HALO, SEQ_TILE), lanes])
    pooled = jnp.concatenate(pooled, axis=1).astype(BF16)
    y_a = _bdot(pooled, poolw_ref[...]) * pscale_ref[...]

    uv = z[HALO:HALO + SEQ_TILE, POOL_WIDTH:]
    uv = 0.5 * uv * (1.0 + lax.erf(uv * (0.5 ** 0.5)))
    u, v = uv[:, :SGU_WIDTH], uv[:, SGU_WIDTH:]
    vn = _layer_norm(v, sg_ref[...], sb_ref[...]).astype(BF16)
    rows = []
    for c in range(SEQ_TILE // CHUNK):
        heads = []
        for hd in range(SGU_HEADS):
            blk = vn[c * CHUNK:(c + 1) * CHUNK, hd * SGU_HEAD_DIM:(hd + 1) * SGU_HEAD_DIM]
            heads.append(_bdot(sguw_ref[hd], blk))
        rows.append(jnp.concatenate(heads, axis=1) + sbias_ref[...])
    y_b = u * jnp.concatenate(rows, axis=0)

    y = _bdot(jnp.concatenate([y_a, y_b], axis=1).astype(BF16), outw_ref[...])
    r = ALPHA * x_ref[0] + gt_ref[0] * y
    o_ref[0] = _layer_norm(r, g_ref[...], b_ref[...])


def _mixer0(x, mods, row_off, in_w, pool_bd, pool_scale, sg, sb, sgu_w, sgu_bias, out_w, g, b):
    bsz, seq, d = x.shape
    prev, main, nxt = _x_specs(seq)
    return pl.pallas_call(
        functools.partial(_mix0_body, seq),
        grid=(bsz, seq // SEQ_TILE),
        in_specs=[prev, main, nxt, _mod_spec(row_off), _mod_spec(row_off), _mod_spec(row_off),
                  _const_spec(in_w.shape), _const_spec(pool_bd.shape), _const_spec(pool_scale.shape),
                  _const_spec(sg.shape), _const_spec(sb.shape), _const_spec(sgu_w.shape),
                  _const_spec(sgu_bias.shape), _const_spec(out_w.shape),
                  _const_spec(g.shape), _const_spec(b.shape)],
        out_specs=pl.BlockSpec((1, SEQ_TILE, d), lambda bb, i: (bb, i, 0)),
        out_shape=jax.ShapeDtypeStruct(x.shape, F32),
        scratch_shapes=[pltpu.VMEM((SEQ_TILE + 2 * HALO, POOL_WIDTH), F32)],
        compiler_params=pltpu.CompilerParams(vmem_limit_bytes=VMEM_LIMIT_BYTES),
        name="mixer0",
    )(x, x, x, mods[0], mods[1], mods[2], in_w, pool_bd, pool_scale, sg, sb, sgu_w, sgu_bias, out_w, g, b)


def _mix1_body(seq, xp_ref, x_ref, xn_ref, sh_ref, sc_ref, gt_ref, pw1_ref, pb1_ref, dww_ref, dwb_ref,
               cg_ref, cb_ref, pw2_ref, pb2_ref, g_ref, b_ref, o_ref, g_s, d_s):
    h_ext = _modulated_ext(xp_ref, x_ref, xn_ref, sh_ref, sc_ref)
    p = _bdot(h_ext, pw1_ref[...]) + pb1_ref[...]
    glu = p[:, :CONV_WIDTH] * _sigmoid(p[:, CONV_WIDTH:])
    g_s[...] = jnp.where(_row_valid(seq), glu, 0.0)

    pad = CONV_TAPS // 2
    for cb in range(CONV_WIDTH // CONV_COLS):
        lanes = slice(cb * CONV_COLS, (cb + 1) * CONV_COLS)
        for rb in range(SEQ_TILE // CONV_ROWS):
            base = HALO - pad + rb * CONV_ROWS
            acc = dwb_ref[:, lanes] + dww_ref[0:1, lanes] * g_s[pl.ds(base, CONV_ROWS), lanes]
            for k in range(1, CONV_TAPS):
                acc = acc + dww_ref[k:k + 1, lanes] * g_s[pl.ds(base + k, CONV_ROWS), lanes]
            d_s[pl.ds(rb * CONV_ROWS, CONV_ROWS), lanes] = acc

    dn = _layer_norm(d_s[...], cg_ref[...], cb_ref[...])
    dn = (dn * _sigmoid(dn)).astype(BF16)
    y = _bdot(dn, pw2_ref[...]) + pb2_ref[...]
    r = ALPHA * x_ref[0] + gt_ref[0] * y
    o_ref[0] = _layer_norm(r, g_ref[...], b_ref[...])


def _mixer1(x, mods, row_off, pw1_w, pw1_b, dw_w, dw_b, cg, cb, pw2_w, pw2_b, g, b):
    bsz, seq, d = x.shape
    prev, main, nxt = _x_specs(seq)
    return pl.pallas_call(
        functools.partial(_mix1_body, seq),
        grid=(bsz, seq // SEQ_TILE),
        in_specs=[prev, main, nxt, _mod_spec(row_off), _mod_spec(row_off), _mod_spec(row_off),
                  _const_spec(pw1_w.shape), _const_spec(pw1_b.shape), _const_spec(dw_w.shape),
                  _const_spec(dw_b.shape), _const_spec(cg.shape), _const_spec(cb.shape),
                  _const_spec(pw2_w.shape), _const_spec(pw2_b.shape),
                  _const_spec(g.shape), _const_spec(b.shape)],
        out_specs=pl.BlockSpec((1, SEQ_TILE, d), lambda bb, i: (bb, i, 0)),
        out_shape=jax.ShapeDtypeStruct(x.shape, F32),
        scratch_shapes=[pltpu.VMEM((SEQ_TILE + 2 * HALO, CONV_WIDTH), F32),
                        pltpu.VMEM((SEQ_TILE, CONV_WIDTH), F32)],
        compiler_params=pltpu.CompilerParams(vmem_limit_bytes=VMEM_LIMIT_BYTES),
        name="mixer1",
    )(x, x, x, mods[0], mods[1], mods[2], pw1_w, pw1_b, dw_w, dw_b, cg, cb, pw2_w, pw2_b, g, b)


def _mlp_body(x_ref, sh_ref, sc_ref, gt_ref, w1_ref, w2_ref, g_ref, b_ref, o_ref):
    x = x_ref[0]
    h = (x * (1.0 + sc_ref[0]) + sh_ref[0]).astype(BF16)
    acc = None
    for f in range(D_FF // FF_CHUNK):
        cols = slice(f * FF_CHUNK, (f + 1) * FF_CHUNK)
        u = jnp.maximum(_bdot(h, w1_ref[:, cols]), 0.0)
        part = _bdot((u * u).astype(BF16), w2_ref[cols, :])
        acc = part if acc is None else acc + part
    r = ALPHA * x + gt_ref[0] * acc
    o_ref[0] = _layer_norm(r, g_ref[...], b_ref[...])


def _mlp(x, mods, row_off, w1, w2, g, b):
    bsz, seq, d = x.shape
    tile = pl.BlockSpec((1, SEQ_TILE, d), lambda bb, i: (bb, i, 0))
    return pl.pallas_call(
        _mlp_body,
        grid=(bsz, seq // SEQ_TILE),
        in_specs=[tile, _mod_spec(row_off), _mod_spec(row_off), _mod_spec(row_off),
                  _const_spec(w1.shape), _const_spec(w2.shape), _const_spec(g.shape), _const_spec(b.shape)],
        out_specs=tile,
        out_shape=jax.ShapeDtypeStruct(x.shape, F32),
        compiler_params=pltpu.CompilerParams(vmem_limit_bytes=VMEM_LIMIT_BYTES),
        name="mlp",
    )(x, mods[3], mods[4], mods[5], w1, w2, g, b)


def _row(v):
    return v.reshape(1, -1)


def kernel(x_prompt, x_sample, c_prompt, c_sample, l0_ada_w, l0_ada_b, l0_in_w, l0_pool_w, l0_pool_scale, l0_sgu_ln_g, l0_sgu_ln_b, l0_sgu_w, l0_sgu_b, l0_out_w, l0_ln1_g, l0_ln1_b, l0_mlp_w1, l0_mlp_w2, l0_ln2_g, l0_ln2_b, l1_ada_w, l1_ada_b, l1_pw1_w, l1_pw1_b, l1_dw_w, l1_dw_b, l1_cnorm_g, l1_cnorm_b, l1_pw2_w, l1_pw2_b, l1_ln1_g, l1_ln1_b, l1_mlp_w1, l1_mlp_w2, l1_ln2_g, l1_ln2_b):
    n_prompt = c_prompt.shape[0]
    c_all = jnp.concatenate([c_prompt, c_sample], axis=0)

    def mods_of(ada_w, ada_b):
        mod = _adaln(c_all, ada_w, ada_b)
        return [m.reshape(-1, 1, D_MODEL) for m in jnp.split(mod, N_MOD, axis=-1)]

    mods0 = mods_of(l0_ada_w, l0_ada_b)
    mods1 = mods_of(l1_ada_w, l1_ada_b)

    eye = jnp.eye(N_POOL_GROUPS, dtype=F32)
    pool_bd = (eye[:, None, :, None] * l0_pool_w[:, :, None, :]).reshape(POOL_WIDTH, POOL_WIDTH).astype(BF16)
    sgu_bias = jnp.repeat(l0_sgu_b.T, SGU_HEAD_DIM, axis=1)

    in_w = l0_in_w.astype(BF16)
    sgu_w = l0_sgu_w.astype(BF16)
    out_w = l0_out_w.astype(BF16)
    w1_0, w2_0 = l0_mlp_w1.astype(BF16), l0_mlp_w2.astype(BF16)
    pw1_w, pw2_w = l1_pw1_w.astype(BF16), l1_pw2_w.astype(BF16)
    w1_1, w2_1 = l1_mlp_w1.astype(BF16), l1_mlp_w2.astype(BF16)

    def trunk(x, row_off):
        x = _mixer0(x, mods0, row_off, in_w, pool_bd, _row(l0_pool_scale), _row(l0_sgu_ln_g), _row(l0_sgu_ln_b),
                    sgu_w, sgu_bias, out_w, _row(l0_ln1_g), _row(l0_ln1_b))
        x = _mlp(x, mods0, row_off, w1_0, w2_0, _row(l0_ln2_g), _row(l0_ln2_b))
        x = _mixer1(x, mods1, row_off, pw1_w, _row(l1_pw1_b), l1_dw_w, _row(l1_dw_b), _row(l1_cnorm_g),
                    _row(l1_cnorm_b), pw2_w, _row(l1_pw2_b), _row(l1_ln1_g), _row(l1_ln1_b))
        x = _mlp(x, mods1, row_off, w1_1, w2_1, _row(l1_ln2_g), _row(l1_ln2_b))
        return x

    return (trunk(x_prompt, 0), trunk(x_sample, n_prompt))
```

```python
import functools

import jax
import jax.numpy as jnp
from jax import lax
from jax.experimental import pallas as pl
from jax.experimental.pallas import tpu as pltpu

D_MODEL = 1024
DEPTH = 2
POOL_WINDOWS = (2, 4, 8, 16)
N_POOL_GROUPS = 4
POOL_GROUP_DIM = 128
POOL_WIDTH = 512
SGU_HEADS = 4
SGU_HEAD_DIM = 128
SGU_WIDTH = 512
CHUNK = 128
IN_WIDTH = POOL_WIDTH + 2 * SGU_WIDTH
CONV_WIDTH = D_MODEL
CONV_TAPS = 31
D_FF = 4 * D_MODEL
N_MOD = 6
ALPHA = (2.0 * DEPTH) ** 0.25
LN_EPS = 1e-5

F32 = jnp.float32
BF16 = jnp.bfloat16

V7X_VMEM_BYTES = 64 * 1024 * 1024
VMEM_LIMIT_BYTES = V7X_VMEM_BYTES - 8 * 1024 * 1024

SEQ_TILE = 512
HALO = 16
FF_CHUNK = 1024
SUBLANES = 8
LANES = 128
LANE_BLOCKS = CONV_WIDTH // LANES
assert LANE_BLOCKS == SUBLANES
CONV_BLOCK = 16
ADA_COLS = 1536


def _layer_norm(r, g, b):
    mu = jnp.mean(r, axis=-1, keepdims=True)
    d = r - mu
    var = jnp.mean(d * d, axis=-1, keepdims=True)
    return d * lax.rsqrt(var + LN_EPS) * g + b


def _sigmoid(x):
    return 1.0 / (1.0 + jnp.exp(-x))


def _bdot(a, b):
    return jnp.dot(a, b, preferred_element_type=F32)


def _ada_body(c_ref, w_ref, b_ref, o_ref):
    c = c_ref[...]
    s = (c * _sigmoid(c)).astype(BF16)
    o_ref[...] = _bdot(s, w_ref[...].astype(BF16)) + b_ref[...]


def _adaln(c, w, b):
    m, d = c.shape
    n = w.shape[1]
    return pl.pallas_call(
        _ada_body,
        grid=(n // ADA_COLS,),
        in_specs=[
            pl.BlockSpec((m, d), lambda j: (0, 0)),
            pl.BlockSpec((d, ADA_COLS), lambda j: (0, j)),
            pl.BlockSpec((1, ADA_COLS), lambda j: (0, j)),
        ],
        out_specs=pl.BlockSpec((m, ADA_COLS), lambda j: (0, j)),
        out_shape=jax.ShapeDtypeStruct((m, n), F32),
        compiler_params=pltpu.CompilerParams(vmem_limit_bytes=VMEM_LIMIT_BYTES),
        name="adaln",
    )(c, w, b.reshape(1, n))


def _const_spec(shape):
    nd = len(shape)
    return pl.BlockSpec(shape, lambda b, i: (0,) * nd)


def _mod_spec(row_off):
    return pl.BlockSpec((1, 1, D_MODEL), lambda b, i: (b + row_off, 0, 0))


def _x_specs(seq):
    per_tile = SEQ_TILE // HALO
    last = seq // HALO - 1
    prev = pl.BlockSpec((1, HALO, D_MODEL), lambda b, i: (b, jnp.maximum(i * per_tile - 1, 0), 0))
    main = pl.BlockSpec((1, SEQ_TILE, D_MODEL), lambda b, i: (b, i, 0))
    nxt = pl.BlockSpec((1, HALO, D_MODEL), lambda b, i: (b, jnp.minimum((i + 1) * per_tile, last), 0))
    return prev, main, nxt


def _modulated_ext(xp_ref, x_ref, xn_ref, sh_ref, sc_ref):
    x_ext = jnp.concatenate([xp_ref[0], x_ref[0], xn_ref[0]], axis=0)
    return (x_ext * (1.0 + sc_ref[0]) + sh_ref[0]).astype(BF16)


def _row_valid(seq):
    i = pl.program_id(1)
    pos = i * SEQ_TILE - HALO + lax.broadcasted_iota(jnp.int32, (SEQ_TILE + 2 * HALO, 1), 0)
    return (pos >= 0) & (pos < seq)


def _mix0_body(seq, xp_ref, x_ref, xn_ref, sh_ref, sc_ref, gt_ref, inw_ref, poolw_ref, pscale_ref,
               sg_ref, sb_ref, sguw_ref, sbias_ref, outw_ref, g_ref, b_ref, o_ref, a_s):
    h_ext = _modulated_ext(xp_ref, x_ref, xn_ref, sh_ref, sc_ref)
    z = _bdot(h_ext, inw_ref[...])
    a_s[...] = jnp.where(_row_valid(seq), z[:, :POOL_WIDTH], 0.0)

    pos = pl.program_id(1) * SEQ_TILE + lax.broadcasted_iota(jnp.int32, (SEQ_TILE, 1), 0)
    pooled = []
    for grp, window in enumerate(POOL_WINDOWS):
        half = window // 2
        lanes = slice(grp * POOL_GROUP_DIM, (grp + 1) * POOL_GROUP_DIM)
        total = a_s[pl.ds(HALO - half, SEQ_TILE), lanes]
        for k in range(-half + 1, half):
            total = total + a_s[pl.ds(HALO + k, SEQ_TILE), lanes]
        cnt = jnp.minimum(pos + half, seq) - jnp.maximum(pos - half, 0)
        pooled.append(total * (1.0 / cnt.astype(F32)) - a_s[pl.ds(HALO, SEQ_TILE), lanes])
    pooled = jnp.concatenate(pooled, axis=1).astype(BF16)
    y_a = _bdot(pooled, poolw_ref[...]) * pscale_ref[...]

    uv = z[HALO:HALO + SEQ_TILE, POOL_WIDTH:]
    uv = 0.5 * uv * (1.0 + lax.erf(uv * (0.5 ** 0.5)))
    u, v = uv[:, :SGU_WIDTH], uv[:, SGU_WIDTH:]
    vn = _layer_norm(v, sg_ref[...], sb_ref[...]).astype(BF16)
    rows = []
    for c in range(SEQ_TILE // CHUNK):
        heads = []
        for hd in range(SGU_HEADS):
            blk = vn[c * CHUNK:(c + 1) * CHUNK, hd * SGU_HEAD_DIM:(hd + 1) * SGU_HEAD_DIM]
            heads.append(_bdot(sguw_ref[hd], blk))
        rows.append(jnp.concatenate(heads, axis=1) + sbias_ref[...])
    y_b = u * jnp.concatenate(rows, axis=0)

    y = _bdot(jnp.concatenate([y_a, y_b], axis=1).astype(BF16), outw_ref[...])
    r = ALPHA * x_ref[0] + gt_ref[0] * y
    o_ref[0] = _layer_norm(r, g_ref[...], b_ref[...])


def _mixer0(x, mods, row_off, in_w, pool_bd, pool_scale, sg, sb, sgu_w, sgu_bias, out_w, g, b):
    bsz, seq, d = x.shape
    prev, main, nxt = _x_specs(seq)
    return pl.pallas_call(
        functools.partial(_mix0_body, seq),
        grid=(bsz, seq // SEQ_TILE),
        in_specs=[prev, main, nxt, _mod_spec(row_off), _mod_spec(row_off), _mod_spec(row_off),
                  _const_spec(in_w.shape), _const_spec(pool_bd.shape), _const_spec(pool_scale.shape),
                  _const_spec(sg.shape), _const_spec(sb.shape), _const_spec(sgu_w.shape),
                  _const_spec(sgu_bias.shape), _const_spec(out_w.shape),
                  _const_spec(g.shape), _const_spec(b.shape)],
        out_specs=pl.BlockSpec((1, SEQ_TILE, d), lambda bb, i: (bb, i, 0)),
        out_shape=jax.ShapeDtypeStruct(x.shape, F32),
        scratch_shapes=[pltpu.VMEM((SEQ_TILE + 2 * HALO, POOL_WIDTH), F32)],
        compiler_params=pltpu.CompilerParams(vmem_limit_bytes=VMEM_LIMIT_BYTES),
        name="mixer0",
    )(x, x, x, mods[0], mods[1], mods[2], in_w, pool_bd, pool_scale, sg, sb, sgu_w, sgu_bias, out_w, g, b)


def _mix1_body(seq, xp_ref, x_ref, xn_ref, sh_ref, sc_ref, gt_ref, pw1_ref, pb1_ref, dww_ref, dwb_ref,
               cg_ref, cb_ref, pw2_ref, pb2_ref, g_ref, b_ref, o_ref, g_s, d_s):
    h_ext = _modulated_ext(xp_ref, x_ref, xn_ref, sh_ref, sc_ref)
    p = _bdot(h_ext, pw1_ref[...]) + pb1_ref[...]
    glu = p[:, :CONV_WIDTH] * _sigmoid(p[:, CONV_WIDTH:])
    glu = jnp.where(_row_valid(seq), glu, 0.0)
    ext = SEQ_TILE + 2 * HALO

    for i in range(ext // SUBLANES):
        for j in range(LANE_BLOCKS):
            g_s[pl.ds(i * SUBLANES * LANE_BLOCKS + j, SUBLANES, stride=LANE_BLOCKS), :] = (
                glu[i * SUBLANES:(i + 1) * SUBLANES, j * LANES:(j + 1) * LANES])

    first = HALO - CONV_TAPS // 2
    for blk in range(SEQ_TILE // CONV_BLOCK):
        t0 = blk * CONV_BLOCK
        accs = [dwb_ref[...]] * CONV_BLOCK
        for k in range(CONV_TAPS):
            wk = dww_ref[k]
            accs = [acc + wk * g_s[pl.ds((t0 + t + first + k) * LANE_BLOCKS, LANE_BLOCKS), :]
                    for t, acc in enumerate(accs)]
        for t, acc in enumerate(accs):
            d_s[pl.ds((t0 + t) * LANE_BLOCKS, LANE_BLOCKS), :] = acc

    d = jnp.concatenate(
        [jnp.concatenate([d_s[pl.ds(i * SUBLANES * LANE_BLOCKS + j, SUBLANES, stride=LANE_BLOCKS), :]
                          for j in range(LANE_BLOCKS)], axis=1)
         for i in range(SEQ_TILE // SUBLANES)], axis=0)
    dn = _layer_norm(d, cg_ref[...], cb_ref[...])
    dn = (dn * _sigmoid(dn)).astype(BF16)
    y = _bdot(dn, pw2_ref[...]) + pb2_ref[...]
    r = ALPHA * x_ref[0] + gt_ref[0] * y
    o_ref[0] = _layer_norm(r, g_ref[...], b_ref[...])


def _mixer1(x, mods, row_off, pw1_w, pw1_b, dw_w, dw_b, cg, cb, pw2_w, pw2_b, g, b):
    bsz, seq, d = x.shape
    prev, main, nxt = _x_specs(seq)
    return pl.pallas_call(
        functools.partial(_mix1_body, seq),
        grid=(bsz, seq // SEQ_TILE),
        in_specs=[prev, main, nxt, _mod_spec(row_off), _mod_spec(row_off), _mod_spec(row_off),
                  _const_spec(pw1_w.shape), _const_spec(pw1_b.shape), _const_spec(dw_w.shape),
                  _const_spec(dw_b.shape), _const_spec(cg.shape), _const_spec(cb.shape),
                  _const_spec(pw2_w.shape), _const_spec(pw2_b.shape),
                  _const_spec(g.shape), _const_spec(b.shape)],
        out_specs=pl.BlockSpec((1, SEQ_TILE, d), lambda bb, i: (bb, i, 0)),
        out_shape=jax.ShapeDtypeStruct(x.shape, F32),
        scratch_shapes=[pltpu.VMEM(((SEQ_TILE + 2 * HALO) * LANE_BLOCKS, LANES), F32),
                        pltpu.VMEM((SEQ_TILE * LANE_BLOCKS, LANES), F32)],
        compiler_params=pltpu.CompilerParams(vmem_limit_bytes=VMEM_LIMIT_BYTES),
        name="mixer1",
    )(x, x, x, mods[0], mods[1], mods[2], pw1_w, pw1_b, dw_w, dw_b, cg, cb, pw2_w, pw2_b, g, b)


def _mlp_body(x_ref, sh_ref, sc_ref, gt_ref, w1_ref, w2_ref, g_ref, b_ref, o_ref):
    x = x_ref[0]
    h = (x * (1.0 + sc_ref[0]) + sh_ref[0]).astype(BF16)
    acc = None
    for f in range(D_FF // FF_CHUNK):
        cols = slice(f * FF_CHUNK, (f + 1) * FF_CHUNK)
        u = jnp.maximum(_bdot(h, w1_ref[:, cols]), 0.0)
        part = _bdot((u * u).astype(BF16), w2_ref[cols, :])
        acc = part if acc is None else acc + part
    r = ALPHA * x + gt_ref[0] * acc
    o_ref[0] = _layer_norm(r, g_ref[...], b_ref[...])


def _mlp(x, mods, row_off, w1, w2, g, b):
    bsz, seq, d = x.shape
    tile = pl.BlockSpec((1, SEQ_TILE, d), lambda bb, i: (bb, i, 0))
    return pl.pallas_call(
        _mlp_body,
        grid=(bsz, seq // SEQ_TILE),
        in_specs=[tile, _mod_spec(row_off), _mod_spec(row_off), _mod_spec(row_off),
                  _const_spec(w1.shape), _const_spec(w2.shape), _const_spec(g.shape), _const_spec(b.shape)],
        out_specs=tile,
        out_shape=jax.ShapeDtypeStruct(x.shape, F32),
        compiler_params=pltpu.CompilerParams(vmem_limit_bytes=VMEM_LIMIT_BYTES),
        name="mlp",
    )(x, mods[3], mods[4], mods[5], w1, w2, g, b)


def _row(v):
    return v.reshape(1, -1)


def kernel(x_prompt, x_sample, c_prompt, c_sample, l0_ada_w, l0_ada_b, l0_in_w, l0_pool_w, l0_pool_scale, l0_sgu_ln_g, l0_sgu_ln_b, l0_sgu_w, l0_sgu_b, l0_out_w, l0_ln1_g, l0_ln1_b, l0_mlp_w1, l0_mlp_w2, l0_ln2_g, l0_ln2_b, l1_ada_w, l1_ada_b, l1_pw1_w, l1_pw1_b, l1_dw_w, l1_dw_b, l1_cnorm_g, l1_cnorm_b, l1_pw2_w, l1_pw2_b, l1_ln1_g, l1_ln1_b, l1_mlp_w1, l1_mlp_w2, l1_ln2_g, l1_ln2_b):
    n_prompt = c_prompt.shape[0]
    c_all = jnp.concatenate([c_prompt, c_sample], axis=0)

    def mods_of(ada_w, ada_b):
        mod = _adaln(c_all, ada_w, ada_b)
        return [m.reshape(-1, 1, D_MODEL) for m in jnp.split(mod, N_MOD, axis=-1)]

    mods0 = mods_of(l0_ada_w, l0_ada_b)
    mods1 = mods_of(l1_ada_w, l1_ada_b)

    eye = jnp.eye(N_POOL_GROUPS, dtype=F32)
    pool_bd = (eye[:, None, :, None] * l0_pool_w[:, :, None, :]).reshape(POOL_WIDTH, POOL_WIDTH).astype(BF16)
    sgu_bias = jnp.repeat(l0_sgu_b.T, SGU_HEAD_DIM, axis=1)

    in_w = l0_in_w.astype(BF16)
    sgu_w = l0_sgu_w.astype(BF16)
    out_w = l0_out_w.astype(BF16)
    w1_0, w2_0 = l0_mlp_w1.astype(BF16), l0_mlp_w2.astype(BF16)
    pw1_w, pw2_w = l1_pw1_w.astype(BF16), l1_pw2_w.astype(BF16)
    w1_1, w2_1 = l1_mlp_w1.astype(BF16), l1_mlp_w2.astype(BF16)

    def trunk(x, row_off):
        x = _mixer0(x, mods0, row_off, in_w, pool_bd, _row(l0_pool_scale), _row(l0_sgu_ln_g), _row(l0_sgu_ln_b),
                    sgu_w, sgu_bias, out_w, _row(l0_ln1_g), _row(l0_ln1_b))
        x = _mlp(x, mods0, row_off, w1_0, w2_0, _row(l0_ln2_g), _row(l0_ln2_b))
        x = _mixer1(x, mods1, row_off, pw1_w, _row(l1_pw1_b), l1_dw_w.reshape(CONV_TAPS, LANE_BLOCKS, LANES),
                    l1_dw_b.reshape(LANE_BLOCKS, LANES), _row(l1_cnorm_g),
                    _row(l1_cnorm_b), pw2_w, _row(l1_pw2_b), _row(l1_ln1_g), _row(l1_ln1_b))
        x = _mlp(x, mods1, row_off, w1_1, w2_1, _row(l1_ln2_g), _row(l1_ln2_b))
        return x

    return (trunk(x_prompt, 0), trunk(x_sample, n_prompt))
```

```python
import functools

import jax
import jax.numpy as jnp
from jax import lax
from jax.experimental import pallas as pl
from jax.experimental.pallas import tpu as pltpu

D_MODEL = 1024
DEPTH = 2
POOL_WINDOWS = (2, 4, 8, 16)
N_POOL_GROUPS = 4
POOL_GROUP_DIM = 128
POOL_WIDTH = 512
SGU_HEADS = 4
SGU_HEAD_DIM = 128
SGU_WIDTH = 512
CHUNK = 128
IN_WIDTH = POOL_WIDTH + 2 * SGU_WIDTH
CONV_WIDTH = D_MODEL
CONV_TAPS = 31
D_FF = 4 * D_MODEL
N_MOD = 6
ALPHA = (2.0 * DEPTH) ** 0.25
LN_EPS = 1e-5

F32 = jnp.float32
BF16 = jnp.bfloat16

V7X_VMEM_BYTES = 64 * 1024 * 1024
VMEM_LIMIT_BYTES = V7X_VMEM_BYTES - 8 * 1024 * 1024

SEQ_TILE = 512
HALO = 16
FF_CHUNK = 1024
MLP_TILE = 1024
MLP_SUB = 256
SUBLANES = 8
LANES = 128
LANE_BLOCKS = CONV_WIDTH // LANES
assert LANE_BLOCKS == SUBLANES
CONV_PAIRS = 16
ADA_COLS = 1536


def _layer_norm(r, g, b):
    mu = jnp.mean(r, axis=-1, keepdims=True)
    d = r - mu
    var = jnp.mean(d * d, axis=-1, keepdims=True)
    return d * lax.rsqrt(var + LN_EPS) * g + b


def _sigmoid(x):
    return 1.0 / (1.0 + jnp.exp(-x))


def _bdot(a, b):
    return jnp.dot(a, b, preferred_element_type=F32)


def _ada_body(c_ref, w_ref, b_ref, o_ref):
    c = c_ref[...]
    s = (c * _sigmoid(c)).astype(BF16)
    o_ref[...] = _bdot(s, w_ref[...].astype(BF16)) + b_ref[...]


def _adaln(c, w, b):
    m, d = c.shape
    n = w.shape[1]
    return pl.pallas_call(
        _ada_body,
        grid=(n // ADA_COLS,),
        in_specs=[
            pl.BlockSpec((m, d), lambda j: (0, 0)),
            pl.BlockSpec((d, ADA_COLS), lambda j: (0, j)),
            pl.BlockSpec((1, ADA_COLS), lambda j: (0, j)),
        ],
        out_specs=pl.BlockSpec((m, ADA_COLS), lambda j: (0, j)),
        out_shape=jax.ShapeDtypeStruct((m, n), F32),
        compiler_params=pltpu.CompilerParams(vmem_limit_bytes=VMEM_LIMIT_BYTES),
        name="adaln",
    )(c, w, b.reshape(1, n))


def _const_spec(shape):
    nd = len(shape)
    return pl.BlockSpec(shape, lambda b, i: (0,) * nd, pipeline_mode=pl.Buffered(1))


def _mod_spec(row_off):
    return pl.BlockSpec((1, 1, D_MODEL), lambda b, i: (b + row_off, 0, 0))


def _x_specs(seq):
    per_tile = SEQ_TILE // HALO
    last = seq // HALO - 1
    prev = pl.BlockSpec((1, HALO, D_MODEL), lambda b, i: (b, jnp.maximum(i * per_tile - 1, 0), 0))
    main = pl.BlockSpec((1, SEQ_TILE, D_MODEL), lambda b, i: (b, i, 0))
    nxt = pl.BlockSpec((1, HALO, D_MODEL), lambda b, i: (b, jnp.minimum((i + 1) * per_tile, last), 0))
    return prev, main, nxt


def _modulated_ext(xp_ref, x_ref, xn_ref, sh_ref, sc_ref):
    x_ext = jnp.concatenate([xp_ref[0], x_ref[0], xn_ref[0]], axis=0)
    return (x_ext * (1.0 + sc_ref[0]) + sh_ref[0]).astype(BF16)


def _row_valid(seq):
    i = pl.program_id(1)
    pos = i * SEQ_TILE - HALO + lax.broadcasted_iota(jnp.int32, (SEQ_TILE + 2 * HALO, 1), 0)
    return (pos >= 0) & (pos < seq)


def _mix0_body(seq, xp_ref, x_ref, xn_ref, sh_ref, sc_ref, gt_ref, inw_ref, poolw_ref, pscale_ref,
               sg_ref, sb_ref, sguw_ref, sbias_ref, outw_ref, g_ref, b_ref, o_ref, a_s):
    h_ext = _modulated_ext(xp_ref, x_ref, xn_ref, sh_ref, sc_ref)
    z = _bdot(h_ext, inw_ref[...])
    a_s[...] = jnp.where(_row_valid(seq), z[:, :POOL_WIDTH], 0.0)

    pos = pl.program_id(1) * SEQ_TILE + lax.broadcasted_iota(jnp.int32, (SEQ_TILE, 1), 0)
    pooled = []
    for grp, window in enumerate(POOL_WINDOWS):
        half = window // 2
        lanes = slice(grp * POOL_GROUP_DIM, (grp + 1) * POOL_GROUP_DIM)
        total = a_s[pl.ds(HALO - half, SEQ_TILE), lanes]
        for k in range(-half + 1, half):
            total = total + a_s[pl.ds(HALO + k, SEQ_TILE), lanes]
        cnt = jnp.minimum(pos + half, seq) - jnp.maximum(pos - half, 0)
        pooled.append(total * (1.0 / cnt.astype(F32)) - a_s[pl.ds(HALO, SEQ_TILE), lanes])
    pooled = jnp.concatenate(pooled, axis=1).astype(BF16)
    y_a = _bdot(pooled, poolw_ref[...]) * pscale_ref[...]

    uv = z[HALO:HALO + SEQ_TILE, POOL_WIDTH:]
    uv = 0.5 * uv * (1.0 + lax.erf(uv * (0.5 ** 0.5)))
    u, v = uv[:, :SGU_WIDTH], uv[:, SGU_WIDTH:]
    vn = _layer_norm(v, sg_ref[...], sb_ref[...]).astype(BF16)
    rows = []
    for c in range(SEQ_TILE // CHUNK):
        heads = []
        for hd in range(SGU_HEADS):
            blk = vn[c * CHUNK:(c + 1) * CHUNK, hd * SGU_HEAD_DIM:(hd + 1) * SGU_HEAD_DIM]
            heads.append(_bdot(sguw_ref[hd], blk))
        rows.append(jnp.concatenate(heads, axis=1) + sbias_ref[...])
    y_b = u * jnp.concatenate(rows, axis=0)

    y = _bdot(jnp.concatenate([y_a, y_b], axis=1).astype(BF16), outw_ref[...])
    r = ALPHA * x_ref[0] + gt_ref[0] * y
    o_ref[0] = _layer_norm(r, g_ref[...], b_ref[...])


def _mixer0(x, mods, row_off, in_w, pool_bd, pool_scale, sg, sb, sgu_w, sgu_bias, out_w, g, b):
    bsz, seq, d = x.shape
    prev, main, nxt = _x_specs(seq)
    return pl.pallas_call(
        functools.partial(_mix0_body, seq),
        grid=(bsz, seq // SEQ_TILE),
        in_specs=[prev, main, nxt, _mod_spec(row_off), _mod_spec(row_off), _mod_spec(row_off),
                  _const_spec(in_w.shape), _const_spec(pool_bd.shape), _const_spec(pool_scale.shape),
                  _const_spec(sg.shape), _const_spec(sb.shape), _const_spec(sgu_w.shape),
                  _const_spec(sgu_bias.shape), _const_spec(out_w.shape),
                  _const_spec(g.shape), _const_spec(b.shape)],
        out_specs=pl.BlockSpec((1, SEQ_TILE, d), lambda bb, i: (bb, i, 0)),
        out_shape=jax.ShapeDtypeStruct(x.shape, F32),
        scratch_shapes=[pltpu.VMEM((SEQ_TILE + 2 * HALO, POOL_WIDTH), F32)],
        compiler_params=pltpu.CompilerParams(vmem_limit_bytes=VMEM_LIMIT_BYTES),
        name="mixer0",
    )(x, x, x, mods[0], mods[1], mods[2], in_w, pool_bd, pool_scale, sg, sb, sgu_w, sgu_bias, out_w, g, b)


def _mix1_body(seq, xp_ref, x_ref, xn_ref, sh_ref, sc_ref, gt_ref, pw1_ref, pb1_ref, dww_ref, dwb_ref,
               cg_ref, cb_ref, pw2_ref, pb2_ref, g_ref, b_ref, o_ref, g_s, ge_s, go_s, d_s):
    h_ext = _modulated_ext(xp_ref, x_ref, xn_ref, sh_ref, sc_ref)
    p = _bdot(h_ext, pw1_ref[...]) + pb1_ref[...]
    glu = p[:, :CONV_WIDTH] * _sigmoid(p[:, CONV_WIDTH:])
    ext = SEQ_TILE + 2 * HALO
    valid = _row_valid(seq)
    glu = jnp.concatenate([jnp.where(valid[:HALO], glu[:HALO], 0.0), glu[HALO:HALO + SEQ_TILE],
                           jnp.where(valid[HALO + SEQ_TILE:], glu[HALO + SEQ_TILE:], 0.0)], axis=0)

    for i in range(ext // SUBLANES):
        for j in range(LANE_BLOCKS):
            g_s[pl.ds(i * SUBLANES * LANE_BLOCKS + j, SUBLANES, stride=LANE_BLOCKS), :] = (
                glu[i * SUBLANES:(i + 1) * SUBLANES, j * LANES:(j + 1) * LANES])

    pair_rows = 2 * LANE_BLOCKS
    gt = g_s[...]
    ge_s[...] = gt.reshape(ext // 2, pair_rows, LANES).astype(BF16)
    go_s[pl.ds(0, ext // 2 - 1)] = (
        gt[LANE_BLOCKS:ext * LANE_BLOCKS - LANE_BLOCKS].reshape(ext // 2 - 1, pair_rows, LANES).astype(BF16))

    first = HALO - CONV_TAPS // 2
    n_tiles = CONV_PAIRS + (first + CONV_TAPS - 1) // 2

    def conv_step(i, carry):
        q0 = i * CONV_PAIRS
        te = [ge_s[q0 + n] for n in range(n_tiles)]
        to = [go_s[q0 + n] for n in range(n_tiles)]
        accs = [None] * CONV_PAIRS
        for k in range(CONV_TAPS):
            off = first + k
            wk = dww_ref[k].astype(F32)
            for q in range(CONV_PAIRS):
                tile = (te if off % 2 == 0 else to)[q + off // 2]
                prod = wk * tile.astype(F32)
                accs[q] = prod if accs[q] is None else accs[q] + prod
        for q in range(CONV_PAIRS):
            row = pl.multiple_of((q0 + q) * pair_rows, pair_rows)
            d_s[pl.ds(row, pair_rows), :] = accs[q] + dwb_ref[...]
        return carry

    lax.fori_loop(0, SEQ_TILE // (2 * CONV_PAIRS), conv_step, 0)

    d = jnp.concatenate(
        [jnp.concatenate([d_s[pl.ds(i * SUBLANES * LANE_BLOCKS + j, SUBLANES, stride=LANE_BLOCKS), :]
                          for j in range(LANE_BLOCKS)], axis=1)
         for i in range(SEQ_TILE // SUBLANES)], axis=0)
    dn = _layer_norm(d, cg_ref[...], cb_ref[...])
    dn = (dn * _sigmoid(dn)).astype(BF16)
    y = _bdot(dn, pw2_ref[...]) + pb2_ref[...]
    r = ALPHA * x_ref[0] + gt_ref[0] * y
    o_ref[0] = _layer_norm(r, g_ref[...], b_ref[...])


def _mixer1(x, mods, row_off, pw1_w, pw1_b, dw_w, dw_b, cg, cb, pw2_w, pw2_b, g, b):
    bsz, seq, d = x.shape
    prev, main, nxt = _x_specs(seq)
    return pl.pallas_call(
        functools.partial(_mix1_body, seq),
        grid=(bsz, seq // SEQ_TILE),
        in_specs=[prev, main, nxt, _mod_spec(row_off), _mod_spec(row_off), _mod_spec(row_off),
                  _const_spec(pw1_w.shape), _const_spec(pw1_b.shape), _const_spec(dw_w.shape),
                  _const_spec(dw_b.shape), _const_spec(cg.shape), _const_spec(cb.shape),
                  _const_spec(pw2_w.shape), _const_spec(pw2_b.shape),
                  _const_spec(g.shape), _const_spec(b.shape)],
        out_specs=pl.BlockSpec((1, SEQ_TILE, d), lambda bb, i: (bb, i, 0)),
        out_shape=jax.ShapeDtypeStruct(x.shape, F32),
        scratch_shapes=[pltpu.VMEM(((SEQ_TILE + 2 * HALO) * LANE_BLOCKS, LANES), F32),
                        pltpu.VMEM(((SEQ_TILE + 2 * HALO) // 2, 2 * LANE_BLOCKS, LANES), BF16),
                        pltpu.VMEM(((SEQ_TILE + 2 * HALO) // 2, 2 * LANE_BLOCKS, LANES), BF16),
                        pltpu.VMEM((SEQ_TILE * LANE_BLOCKS, LANES), F32)],
        compiler_params=pltpu.CompilerParams(vmem_limit_bytes=VMEM_LIMIT_BYTES),
        name="mixer1",
    )(x, x, x, mods[0], mods[1], mods[2], pw1_w, pw1_b, dw_w, dw_b, cg, cb, pw2_w, pw2_b, g, b)


def _mlp_body(x_ref, sh_ref, sc_ref, gt_ref, w1_ref, w2_ref, g_ref, b_ref, o_ref):
    for s in range(MLP_TILE // MLP_SUB):
        rows = pl.ds(s * MLP_SUB, MLP_SUB)
        x = x_ref[0, rows, :]
        h = (x * (1.0 + sc_ref[0]) + sh_ref[0]).astype(BF16)
        acc = None
        for f in range(D_FF // FF_CHUNK):
            cols = slice(f * FF_CHUNK, (f + 1) * FF_CHUNK)
            u = jnp.maximum(_bdot(h, w1_ref[:, cols]), 0.0)
            part = _bdot((u * u).astype(BF16), w2_ref[cols, :])
            acc = part if acc is None else acc + part
        r = ALPHA * x + gt_ref[0] * acc
        o_ref[0, rows, :] = _layer_norm(r, g_ref[...], b_ref[...])


def _mlp(x, mods, row_off, w1, w2, g, b):
    bsz, seq, d = x.shape
    tile = pl.BlockSpec((1, MLP_TILE, d), lambda bb, i: (bb, i, 0))
    return pl.pallas_call(
        _mlp_body,
        grid=(bsz, seq // MLP_TILE),
        in_specs=[tile, _mod_spec(row_off), _mod_spec(row_off), _mod_spec(row_off),
                  _const_spec(w1.shape), _const_spec(w2.shape), _const_spec(g.shape), _const_spec(b.shape)],
        out_specs=tile,
        out_shape=jax.ShapeDtypeStruct(x.shape, F32),
        compiler_params=pltpu.CompilerParams(vmem_limit_bytes=VMEM_LIMIT_BYTES),
        name="mlp",
    )(x, mods[3], mods[4], mods[5], w1, w2, g, b)


def _row(v):
    return v.reshape(1, -1)


def kernel(x_prompt, x_sample, c_prompt, c_sample, l0_ada_w, l0_ada_b, l0_in_w, l0_pool_w, l0_pool_scale, l0_sgu_ln_g, l0_sgu_ln_b, l0_sgu_w, l0_sgu_b, l0_out_w, l0_ln1_g, l0_ln1_b, l0_mlp_w1, l0_mlp_w2, l0_ln2_g, l0_ln2_b, l1_ada_w, l1_ada_b, l1_pw1_w, l1_pw1_b, l1_dw_w, l1_dw_b, l1_cnorm_g, l1_cnorm_b, l1_pw2_w, l1_pw2_b, l1_ln1_g, l1_ln1_b, l1_mlp_w1, l1_mlp_w2, l1_ln2_g, l1_ln2_b):
    n_prompt = c_prompt.shape[0]
    c_all = jnp.concatenate([c_prompt, c_sample], axis=0)

    def mods_of(ada_w, ada_b):
        mod = _adaln(c_all, ada_w, ada_b)
        return [m.reshape(-1, 1, D_MODEL) for m in jnp.split(mod, N_MOD, axis=-1)]

    mods0 = mods_of(l0_ada_w, l0_ada_b)
    mods1 = mods_of(l1_ada_w, l1_ada_b)

    eye = jnp.eye(N_POOL_GROUPS, dtype=F32)
    pool_bd = (eye[:, None, :, None] * l0_pool_w[:, :, None, :]).reshape(POOL_WIDTH, POOL_WIDTH).astype(BF16)
    sgu_bias = jnp.repeat(l0_sgu_b.T, SGU_HEAD_DIM, axis=1)

    in_w = l0_in_w.astype(BF16)
    sgu_w = l0_sgu_w.astype(BF16)
    out_w = l0_out_w.astype(BF16)
    w1_0, w2_0 = l0_mlp_w1.astype(BF16), l0_mlp_w2.astype(BF16)
    pw1_w, pw2_w = l1_pw1_w.astype(BF16), l1_pw2_w.astype(BF16)
    dw_pair = jnp.tile(l1_dw_w.reshape(CONV_TAPS, 1, LANE_BLOCKS, LANES), (1, 2, 1, 1))
    dw_pair = dw_pair.reshape(CONV_TAPS, 2 * LANE_BLOCKS, LANES).astype(BF16)
    dwb_pair = jnp.tile(l1_dw_b.reshape(LANE_BLOCKS, LANES), (2, 1))
    w1_1, w2_1 = l1_mlp_w1.astype(BF16), l1_mlp_w2.astype(BF16)

    def trunk(x, row_off):
        x = _mixer0(x, mods0, row_off, in_w, pool_bd, _row(l0_pool_scale), _row(l0_sgu_ln_g), _row(l0_sgu_ln_b),
                    sgu_w, sgu_bias, out_w, _row(l0_ln1_g), _row(l0_ln1_b))
        x = _mlp(x, mods0, row_off, w1_0, w2_0, _row(l0_ln2_g), _row(l0_ln2_b))
        x = _mixer1(x, mods1, row_off, pw1_w, _row(l1_pw1_b), dw_pair, dwb_pair, _row(l1_cnorm_g),
                    _row(l1_cnorm_b), pw2_w, _row(l1_pw2_b), _row(l1_ln1_g), _row(l1_ln1_b))
        x = _mlp(x, mods1, row_off, w1_1, w2_1, _row(l1_ln2_g), _row(l1_ln2_b))
        return x

    return (trunk(x_prompt, 0), trunk(x_sample, n_prompt))
```

```python
import functools

import jax
import jax.numpy as jnp
from jax import lax
from jax.experimental import pallas as pl
from jax.experimental.pallas import tpu as pltpu

D_MODEL = 1024
DEPTH = 2
POOL_WINDOWS = (2, 4, 8, 16)
N_POOL_GROUPS = 4
POOL_GROUP_DIM = 128
POOL_WIDTH = 512
SGU_HEADS = 4
SGU_HEAD_DIM = 128
SGU_WIDTH = 512
CHUNK = 128
IN_WIDTH = POOL_WIDTH + 2 * SGU_WIDTH
CONV_WIDTH = D_MODEL
CONV_TAPS = 31
D_FF = 4 * D_MODEL
N_MOD = 6
ALPHA = (2.0 * DEPTH) ** 0.25
LN_EPS = 1e-5

F32 = jnp.float32
BF16 = jnp.bfloat16

V7X_VMEM_BYTES = 64 * 1024 * 1024
VMEM_LIMIT_BYTES = V7X_VMEM_BYTES - 8 * 1024 * 1024

SEQ_TILE = 512
HALO = 16
FF_CHUNK = 1024
MLP_TILE = 1024
MLP_SUB = 256
MIX_SUB = 256
SUBLANES = 8
LANES = 128
LANE_BLOCKS = CONV_WIDTH // LANES
assert LANE_BLOCKS == SUBLANES
CONV_PAIRS = 16
GLU_COLS = 256
ADA_COLS = 1536


def _layer_norm(r, g, b):
    mu = jnp.mean(r, axis=-1, keepdims=True)
    d = r - mu
    var = jnp.mean(d * d, axis=-1, keepdims=True)
    return d * lax.rsqrt(var + LN_EPS) * g + b


def _sigmoid(x):
    return 1.0 / (1.0 + jnp.exp(-x))


def _bdot(a, b):
    return jnp.dot(a, b, preferred_element_type=F32)


def _ada_body(c_ref, w_ref, b_ref, o_ref):
    c = c_ref[...]
    s = (c * _sigmoid(c)).astype(BF16)
    o_ref[...] = _bdot(s, w_ref[...].astype(BF16)) + b_ref[...]


def _adaln(c, w, b):
    m, d = c.shape
    n = w.shape[1]
    return pl.pallas_call(
        _ada_body,
        grid=(n // ADA_COLS,),
        in_specs=[
            pl.BlockSpec((m, d), lambda j: (0, 0)),
            pl.BlockSpec((d, ADA_COLS), lambda j: (0, j)),
            pl.BlockSpec((1, ADA_COLS), lambda j: (0, j)),
        ],
        out_specs=pl.BlockSpec((m, ADA_COLS), lambda j: (0, j)),
        out_shape=jax.ShapeDtypeStruct((m, n), F32),
        compiler_params=pltpu.CompilerParams(vmem_limit_bytes=VMEM_LIMIT_BYTES),
        name="adaln",
    )(c, w, b.reshape(1, n))


def _const_spec(shape):
    nd = len(shape)
    return pl.BlockSpec(shape, lambda b, i: (0,) * nd, pipeline_mode=pl.Buffered(1))


def _mod_spec(row_off):
    return pl.BlockSpec((1, 1, D_MODEL), lambda b, i: (b + row_off, 0, 0))


def _x_specs(seq):
    per_tile = SEQ_TILE // HALO
    last = seq // HALO - 1
    prev = pl.BlockSpec((1, HALO, D_MODEL), lambda b, i: (b, jnp.maximum(i * per_tile - 1, 0), 0))
    main = pl.BlockSpec((1, SEQ_TILE, D_MODEL), lambda b, i: (b, i, 0))
    nxt = pl.BlockSpec((1, HALO, D_MODEL), lambda b, i: (b, jnp.minimum((i + 1) * per_tile, last), 0))
    return prev, main, nxt


def _modulated_ext(xp_ref, x_ref, xn_ref, sh_ref, sc_ref):
    x_ext = jnp.concatenate([xp_ref[0], x_ref[0], xn_ref[0]], axis=0)
    return (x_ext * (1.0 + sc_ref[0]) + sh_ref[0]).astype(BF16)


def _row_valid(seq):
    i = pl.program_id(1)
    pos = i * SEQ_TILE - HALO + lax.broadcasted_iota(jnp.int32, (SEQ_TILE + 2 * HALO, 1), 0)
    return (pos >= 0) & (pos < seq)


def _mix0_body(seq, xp_ref, x_ref, xn_ref, sh_ref, sc_ref, gt_ref, inw_ref,
               band_ref, sg_ref, sb_ref, sguw_ref, sbias_ref, outw_ref, g_ref, b_ref, o_ref):
    tile_pos = pl.program_id(1) * SEQ_TILE
    h_ext = _modulated_ext(xp_ref, x_ref, xn_ref, sh_ref, sc_ref)
    z = _bdot(h_ext, inw_ref[...])
    valid = _row_valid(seq)

    def inv_count(row0, half):
        pos = tile_pos + row0 + lax.broadcasted_iota(jnp.int32, (HALO, 1), 0)
        cnt = jnp.minimum(pos + half, seq) - jnp.maximum(pos - half, 0)
        return 1.0 / cnt.astype(F32)

    def pooled_block(r0, a_ext):
        if r0 == 0:
            a_ext = jnp.concatenate([jnp.where(valid[:HALO], a_ext[:HALO], 0.0), a_ext[HALO:]], axis=0)
        if r0 + MIX_SUB == SEQ_TILE:
            a_ext = jnp.concatenate([a_ext[:HALO + MIX_SUB],
                                     jnp.where(valid[HALO + SEQ_TILE:], a_ext[HALO + MIX_SUB:], 0.0)], axis=0)
        a_bf = a_ext.astype(BF16)
        starts = range(0, MIX_SUB, CHUNK)
        groups = []
        for grp, window in enumerate(POOL_WINDOWS):
            half = window // 2
            lanes = slice(grp * POOL_GROUP_DIM, (grp + 1) * POOL_GROUP_DIM)
            windows = jnp.concatenate([a_bf[t0:t0 + CHUNK + 2 * HALO, lanes] for t0 in starts], axis=1)
            totals = _bdot(band_ref[grp], windows)
            rows = []
            for c, t0 in enumerate(starts):
                total = totals[:, c * POOL_GROUP_DIM:(c + 1) * POOL_GROUP_DIM]
                centre = a_ext[HALO + t0:HALO + t0 + CHUNK, lanes]
                pooled = total * (1.0 / window) - centre
                if r0 + t0 == 0:
                    head = total[:HALO] * inv_count(0, half) - centre[:HALO]
                    pooled = jnp.concatenate([head, pooled[HALO:]], axis=0)
                if r0 + t0 + CHUNK == SEQ_TILE:
                    tail = total[CHUNK - HALO:] * inv_count(SEQ_TILE - HALO, half) - centre[CHUNK - HALO:]
                    pooled = jnp.concatenate([pooled[:CHUNK - HALO], tail], axis=0)
                rows.append(pooled)
            groups.append(jnp.concatenate(rows, axis=0))
        return jnp.concatenate(groups, axis=1)

    def gate_block(vn):
        n_chunks = MIX_SUB // CHUNK
        mixed = []
        for hd in range(SGU_HEADS):
            lanes = slice(hd * SGU_HEAD_DIM, (hd + 1) * SGU_HEAD_DIM)
            blocks = jnp.concatenate([vn[c * CHUNK:(c + 1) * CHUNK, lanes] for c in range(n_chunks)], axis=1)
            mixed.append(_bdot(sguw_ref[hd], blocks))
        return jnp.concatenate(
            [jnp.concatenate([m[:, c * SGU_HEAD_DIM:(c + 1) * SGU_HEAD_DIM] for m in mixed], axis=1) + sbias_ref[...]
             for c in range(n_chunks)], axis=0)

    for r0 in range(0, SEQ_TILE, MIX_SUB):
        pooled = pooled_block(r0, z[r0:r0 + MIX_SUB + 2 * HALO, :POOL_WIDTH])
        uv = z[HALO + r0:HALO + r0 + MIX_SUB, POOL_WIDTH:]
        uv = 0.5 * uv * (1.0 + lax.erf(uv * (0.5 ** 0.5)))
        u, v = uv[:, :SGU_WIDTH], uv[:, SGU_WIDTH:]
        gated = u * gate_block(_layer_norm(v, sg_ref[...], sb_ref[...]).astype(BF16))
        y = _bdot(jnp.concatenate([pooled, gated], axis=1).astype(BF16), outw_ref[...])
        r = ALPHA * x_ref[0, pl.ds(r0, MIX_SUB), :] + gt_ref[0] * y
        o_ref[0, pl.ds(r0, MIX_SUB), :] = _layer_norm(r, g_ref[...], b_ref[...])


def _pool_bands():
    p = lax.broadcasted_iota(jnp.int32, (CHUNK, CHUNK + 2 * HALO), 0) + HALO
    j = lax.broadcasted_iota(jnp.int32, (CHUNK, CHUNK + 2 * HALO), 1)
    return jnp.stack([((j >= p - w // 2) & (j < p + w // 2)).astype(BF16) for w in POOL_WINDOWS])


def _fold_body(pw_ref, ps_ref, ow_ref, o_ref):
    o_ref[...] = jnp.dot(pw_ref[0] * ps_ref[0], ow_ref[...], preferred_element_type=F32,
                         precision=lax.Precision.HIGHEST)


def _fold_pool_into_out(pool_w, pool_scale, out_w):
    d = out_w.shape[1]
    return pl.pallas_call(
        _fold_body,
        grid=(N_POOL_GROUPS,),
        in_specs=[pl.BlockSpec((1, POOL_GROUP_DIM, POOL_GROUP_DIM), lambda g: (g, 0, 0)),
                  pl.BlockSpec((1, 1, POOL_GROUP_DIM), lambda g: (g, 0, 0)),
                  pl.BlockSpec((POOL_GROUP_DIM, d), lambda g: (g, 0))],
        out_specs=pl.BlockSpec((POOL_GROUP_DIM, d), lambda g: (g, 0)),
        out_shape=jax.ShapeDtypeStruct((POOL_WIDTH, d), F32),
        name="fold_pool",
    )(pool_w, pool_scale.reshape(N_POOL_GROUPS, 1, POOL_GROUP_DIM), out_w)


def _mixer0(x, mods, row_off, in_w, sg, sb, sgu_w, sgu_bias, out_w, g, b):
    bsz, seq, d = x.shape
    prev, main, nxt = _x_specs(seq)
    band = _pool_bands()
    return pl.pallas_call(
        functools.partial(_mix0_body, seq),
        grid=(bsz, seq // SEQ_TILE),
        in_specs=[prev, main, nxt, _mod_spec(row_off), _mod_spec(row_off), _mod_spec(row_off),
                  _const_spec(in_w.shape),
                  _const_spec(band.shape), _const_spec(sg.shape), _const_spec(sb.shape), _const_spec(sgu_w.shape),
                  _const_spec(sgu_bias.shape), _const_spec(out_w.shape),
                  _const_spec(g.shape), _const_spec(b.shape)],
        out_specs=pl.BlockSpec((1, SEQ_TILE, d), lambda bb, i: (bb, i, 0)),
        out_shape=jax.ShapeDtypeStruct(x.shape, F32),
        compiler_params=pltpu.CompilerParams(vmem_limit_bytes=VMEM_LIMIT_BYTES),
        name="mixer0",
    )(x, x, x, mods[0], mods[1], mods[2], in_w, band, sg, sb, sgu_w, sgu_bias, out_w, g, b)


def _mix1_body(seq, xp_ref, x_ref, xn_ref, sh_ref, sc_ref, gt_ref, pw1_ref, pb1_ref, dww_ref, dwb_ref,
               cg_ref, cb_ref, pw2_ref, pb2_ref, g_ref, b_ref, o_ref, g_s, ge_s, go_s, d_s):
    h_ext = _modulated_ext(xp_ref, x_ref, xn_ref, sh_ref, sc_ref)
    ext = SEQ_TILE + 2 * HALO
    valid = _row_valid(seq)

    for blk in range(CONV_WIDTH // GLU_COLS):
        cols = slice(2 * blk * GLU_COLS, 2 * (blk + 1) * GLU_COLS)
        p = _bdot(h_ext, pw1_ref[:, cols]) + pb1_ref[:, cols]
        glu = p[:, :GLU_COLS] * _sigmoid(p[:, GLU_COLS:])
        glu = jnp.concatenate([jnp.where(valid[:HALO], glu[:HALO], 0.0), glu[HALO:HALO + SEQ_TILE],
                               jnp.where(valid[HALO + SEQ_TILE:], glu[HALO + SEQ_TILE:], 0.0)], axis=0)
        for i in range(ext // SUBLANES):
            for jj in range(GLU_COLS // LANES):
                j = blk * (GLU_COLS // LANES) + jj
                g_s[pl.ds(i * SUBLANES * LANE_BLOCKS + j, SUBLANES, stride=LANE_BLOCKS), :] = (
                    glu[i * SUBLANES:(i + 1) * SUBLANES, jj * LANES:(jj + 1) * LANES])

    pair_rows = 2 * LANE_BLOCKS
    gt = g_s[...]
    ge_s[...] = gt.reshape(ext // 2, pair_rows, LANES).astype(BF16)
    go_s[pl.ds(0, ext // 2 - 1)] = (
        gt[LANE_BLOCKS:ext * LANE_BLOCKS - LANE_BLOCKS].reshape(ext // 2 - 1, pair_rows, LANES).astype(BF16))

    first = HALO - CONV_TAPS // 2
    n_tiles = CONV_PAIRS + (first + CONV_TAPS - 1) // 2

    def conv_step(i, carry):
        q0 = i * CONV_PAIRS
        te = [ge_s[q0 + n] for n in range(n_tiles)]
        to = [go_s[q0 + n] for n in range(n_tiles)]
        accs = [None] * CONV_PAIRS
        for k in range(CONV_TAPS):
            off = first + k
            wk = dww_ref[k].astype(F32)
            for q in range(CONV_PAIRS):
                tile = (te if off % 2 == 0 else to)[q + off // 2]
                prod = wk * tile.astype(F32)
                accs[q] = prod if accs[q] is None else accs[q] + prod
        for q in range(CONV_PAIRS):
            row = pl.multiple_of((q0 + q) * pair_rows, pair_rows)
            d_s[pl.ds(row, pair_rows), :] = accs[q] + dwb_ref[...]
        return carry

    lax.fori_loop(0, SEQ_TILE // (2 * CONV_PAIRS), conv_step, 0)

    for r0 in range(0, SEQ_TILE, MIX_SUB):
        d = jnp.concatenate(
            [jnp.concatenate([d_s[pl.ds(i * SUBLANES * LANE_BLOCKS + j, SUBLANES, stride=LANE_BLOCKS), :]
                              for j in range(LANE_BLOCKS)], axis=1)
             for i in range(r0 // SUBLANES, (r0 + MIX_SUB) // SUBLANES)], axis=0)
        dn = _layer_norm(d, cg_ref[...], cb_ref[...])
        dn = (dn * _sigmoid(dn)).astype(BF16)
        y = _bdot(dn, pw2_ref[...]) + pb2_ref[...]
        r = ALPHA * x_ref[0, pl.ds(r0, MIX_SUB), :] + gt_ref[0] * y
        o_ref[0, pl.ds(r0, MIX_SUB), :] = _layer_norm(r, g_ref[...], b_ref[...])


def _mixer1(x, mods, row_off, pw1_w, pw1_b, dw_w, dw_b, cg, cb, pw2_w, pw2_b, g, b):
    bsz, seq, d = x.shape
    prev, main, nxt = _x_specs(seq)
    return pl.pallas_call(
        functools.partial(_mix1_body, seq),
        grid=(bsz, seq // SEQ_TILE),
        in_specs=[prev, main, nxt, _mod_spec(row_off), _mod_spec(row_off), _mod_spec(row_off),
                  _const_spec(pw1_w.shape), _const_spec(pw1_b.shape), _const_spec(dw_w.shape),
                  _const_spec(dw_b.shape), _const_spec(cg.shape), _const_spec(cb.shape),
                  _const_spec(pw2_w.shape), _const_spec(pw2_b.shape),
                  _const_spec(g.shape), _const_spec(b.shape)],
        out_specs=pl.BlockSpec((1, SEQ_TILE, d), lambda bb, i: (bb, i, 0)),
        out_shape=jax.ShapeDtypeStruct(x.shape, F32),
        scratch_shapes=[pltpu.VMEM(((SEQ_TILE + 2 * HALO) * LANE_BLOCKS, LANES), F32),
                        pltpu.VMEM(((SEQ_TILE + 2 * HALO) // 2, 2 * LANE_BLOCKS, LANES), BF16),
                        pltpu.VMEM(((SEQ_TILE + 2 * HALO) // 2, 2 * LANE_BLOCKS, LANES), BF16),
                        pltpu.VMEM((SEQ_TILE * LANE_BLOCKS, LANES), F32)],
        compiler_params=pltpu.CompilerParams(vmem_limit_bytes=VMEM_LIMIT_BYTES),
        name="mixer1",
    )(x, x, x, mods[0], mods[1], mods[2], pw1_w, pw1_b, dw_w, dw_b, cg, cb, pw2_w, pw2_b, g, b)


def _mlp_body(x_ref, sh_ref, sc_ref, gt_ref, w1_ref, w2_ref, g_ref, b_ref, o_ref):
    for s in range(MLP_TILE // MLP_SUB):
        rows = pl.ds(s * MLP_SUB, MLP_SUB)
        x = x_ref[0, rows, :]
        h = (x * (1.0 + sc_ref[0]) + sh_ref[0]).astype(BF16)
        acc = None
        for f in range(D_FF // FF_CHUNK):
            cols = slice(f * FF_CHUNK, (f + 1) * FF_CHUNK)
            u = jnp.maximum(_bdot(h, w1_ref[:, cols]), 0.0)
            part = _bdot((u * u).astype(BF16), w2_ref[cols, :])
            acc = part if acc is None else acc + part
        r = ALPHA * x + gt_ref[0] * acc
        o_ref[0, rows, :] = _layer_norm(r, g_ref[...], b_ref[...])


def _mlp(x, mods, row_off, w1, w2, g, b):
    bsz, seq, d = x.shape
    tile = pl.BlockSpec((1, MLP_TILE, d), lambda bb, i: (bb, i, 0))
    return pl.pallas_call(
        _mlp_body,
        grid=(bsz, seq // MLP_TILE),
        in_specs=[tile, _mod_spec(row_off), _mod_spec(row_off), _mod_spec(row_off),
                  _const_spec(w1.shape), _const_spec(w2.shape), _const_spec(g.shape), _const_spec(b.shape)],
        out_specs=tile,
        out_shape=jax.ShapeDtypeStruct(x.shape, F32),
        compiler_params=pltpu.CompilerParams(vmem_limit_bytes=VMEM_LIMIT_BYTES),
        name="mlp",
    )(x, mods[3], mods[4], mods[5], w1, w2, g, b)


def _row(v):
    return v.reshape(1, -1)


def _glu_pairs(w):
    lead = w.shape[0]
    w = w.reshape(lead, 2, CONV_WIDTH // GLU_COLS, GLU_COLS)
    return jnp.swapaxes(w, 1, 2).reshape(lead, 2 * CONV_WIDTH)


def kernel(x_prompt, x_sample, c_prompt, c_sample, l0_ada_w, l0_ada_b, l0_in_w, l0_pool_w, l0_pool_scale, l0_sgu_ln_g, l0_sgu_ln_b, l0_sgu_w, l0_sgu_b, l0_out_w, l0_ln1_g, l0_ln1_b, l0_mlp_w1, l0_mlp_w2, l0_ln2_g, l0_ln2_b, l1_ada_w, l1_ada_b, l1_pw1_w, l1_pw1_b, l1_dw_w, l1_dw_b, l1_cnorm_g, l1_cnorm_b, l1_pw2_w, l1_pw2_b, l1_ln1_g, l1_ln1_b, l1_mlp_w1, l1_mlp_w2, l1_ln2_g, l1_ln2_b):
    n_prompt = c_prompt.shape[0]
    c_all = jnp.concatenate([c_prompt, c_sample], axis=0)

    def mods_of(ada_w, ada_b):
        mod = _adaln(c_all, ada_w, ada_b)
        return [m.reshape(-1, 1, D_MODEL) for m in jnp.split(mod, N_MOD, axis=-1)]

    mods0 = mods_of(l0_ada_w, l0_ada_b)
    mods1 = mods_of(l1_ada_w, l1_ada_b)

    sgu_bias = jnp.repeat(l0_sgu_b.T, SGU_HEAD_DIM, axis=1)

    in_w = l0_in_w.astype(BF16)
    sgu_w = l0_sgu_w.astype(BF16)
    out_w = jnp.concatenate([_fold_pool_into_out(l0_pool_w, l0_pool_scale, l0_out_w), l0_out_w[POOL_WIDTH:]],
                            axis=0).astype(BF16)
    w1_0, w2_0 = l0_mlp_w1.astype(BF16), l0_mlp_w2.astype(BF16)
    pw1_w, pw1_b = _glu_pairs(l1_pw1_w).astype(BF16), _glu_pairs(_row(l1_pw1_b))
    pw2_w = l1_pw2_w.astype(BF16)
    dw_pair = jnp.tile(l1_dw_w.reshape(CONV_TAPS, 1, LANE_BLOCKS, LANES), (1, 2, 1, 1))
    dw_pair = dw_pair.reshape(CONV_TAPS, 2 * LANE_BLOCKS, LANES).astype(BF16)
    dwb_pair = jnp.tile(l1_dw_b.reshape(LANE_BLOCKS, LANES), (2, 1))
    w1_1, w2_1 = l1_mlp_w1.astype(BF16), l1_mlp_w2.astype(BF16)

    def trunk(x, row_off):
        x = _mixer0(x, mods0, row_off, in_w, _row(l0_sgu_ln_g), _row(l0_sgu_ln_b),
                    sgu_w, sgu_bias, out_w, _row(l0_ln1_g), _row(l0_ln1_b))
        x = _mlp(x, mods0, row_off, w1_0, w2_0, _row(l0_ln2_g), _row(l0_ln2_b))
        x = _mixer1(x, mods1, row_off, pw1_w, pw1_b, dw_pair, dwb_pair, _row(l1_cnorm_g),
                    _row(l1_cnorm_b), pw2_w, _row(l1_pw2_b), _row(l1_ln1_g), _row(l1_ln1_b))
        x = _mlp(x, mods1, row_off, w1_1, w2_1, _row(l1_ln2_g), _row(l1_ln2_b))
        return x

    return (trunk(x_prompt, 0), trunk(x_sample, n_prompt))
```

```python
import functools

import jax
import jax.numpy as jnp
from jax import lax
from jax.experimental import pallas as pl
from jax.experimental.pallas import tpu as pltpu

D_MODEL = 1024
DEPTH = 2
POOL_WINDOWS = (2, 4, 8, 16)
N_POOL_GROUPS = 4
POOL_GROUP_DIM = 128
POOL_WIDTH = 512
SGU_HEADS = 4
SGU_HEAD_DIM = 128
SGU_WIDTH = 512
CHUNK = 128
IN_WIDTH = POOL_WIDTH + 2 * SGU_WIDTH
CONV_WIDTH = D_MODEL
CONV_TAPS = 31
D_FF = 4 * D_MODEL
N_MOD = 6
ALPHA = (2.0 * DEPTH) ** 0.25
LN_EPS = 1e-5

F32 = jnp.float32
BF16 = jnp.bfloat16

V7X_VMEM_BYTES = 64 * 1024 * 1024
VMEM_LIMIT_BYTES = V7X_VMEM_BYTES - 8 * 1024 * 1024

SEQ_TILE = 1024
HALO = 16
FF_CHUNK = 1024
MLP_TILE = 1024
MLP_SUB = 256
MIX_SUB = 256
SUBLANES = 8
LANES = 128
LANE_BLOCKS = CONV_WIDTH // LANES
assert LANE_BLOCKS == SUBLANES
CONV_PAIRS = 16
GLU_COLS = 256
ADA_COLS = 1536


def _layer_norm(r, g, b):
    mu = jnp.mean(r, axis=-1, keepdims=True)
    d = r - mu
    var = jnp.mean(d * d, axis=-1, keepdims=True)
    return d * lax.rsqrt(var + LN_EPS) * g + b


def _sigmoid(x):
    return 1.0 / (1.0 + jnp.exp(-x))


def _bdot(a, b):
    return jnp.dot(a, b, preferred_element_type=F32)


def _ada_body(c_ref, w_ref, b_ref, o_ref):
    c = c_ref[...]
    s = (c * _sigmoid(c)).astype(BF16)
    o_ref[...] = _bdot(s, w_ref[...].astype(BF16)) + b_ref[...]


def _adaln(c, w, b):
    m, d = c.shape
    n = w.shape[1]
    return pl.pallas_call(
        _ada_body,
        grid=(n // ADA_COLS,),
        in_specs=[
            pl.BlockSpec((m, d), lambda j: (0, 0)),
            pl.BlockSpec((d, ADA_COLS), lambda j: (0, j)),
            pl.BlockSpec((1, ADA_COLS), lambda j: (0, j)),
        ],
        out_specs=pl.BlockSpec((m, ADA_COLS), lambda j: (0, j)),
        out_shape=jax.ShapeDtypeStruct((m, n), F32),
        compiler_params=pltpu.CompilerParams(vmem_limit_bytes=VMEM_LIMIT_BYTES),
        name="adaln",
    )(c, w, b.reshape(1, n))


def _const_spec(shape):
    nd = len(shape)
    return pl.BlockSpec(shape, lambda b, i: (0,) * nd, pipeline_mode=pl.Buffered(1))


def _mod_spec(row_off):
    return pl.BlockSpec((1, 1, D_MODEL), lambda b, i: (b + row_off, 0, 0))


def _x_specs(seq):
    per_tile = SEQ_TILE // HALO
    last = seq // HALO - 1
    prev = pl.BlockSpec((1, HALO, D_MODEL), lambda b, i: (b, jnp.maximum(i * per_tile - 1, 0), 0))
    main = pl.BlockSpec((1, SEQ_TILE, D_MODEL), lambda b, i: (b, i, 0))
    nxt = pl.BlockSpec((1, HALO, D_MODEL), lambda b, i: (b, jnp.minimum((i + 1) * per_tile, last), 0))
    return prev, main, nxt


def _modulated_ext(xp_ref, x_ref, xn_ref, sh_ref, sc_ref):
    x_ext = jnp.concatenate([xp_ref[0], x_ref[0], xn_ref[0]], axis=0)
    return (x_ext * (1.0 + sc_ref[0]) + sh_ref[0]).astype(BF16)


def _row_valid(seq):
    i = pl.program_id(1)
    pos = i * SEQ_TILE - HALO + lax.broadcasted_iota(jnp.int32, (SEQ_TILE + 2 * HALO, 1), 0)
    return (pos >= 0) & (pos < seq)


def _mix0_body(seq, xp_ref, x_ref, xn_ref, sh_ref, sc_ref, gt_ref, inw_ref,
               band_ref, sg_ref, sb_ref, sguw_ref, sbias_ref, outw_ref, g_ref, b_ref, o_ref):
    tile_pos = pl.program_id(1) * SEQ_TILE
    h_ext = _modulated_ext(xp_ref, x_ref, xn_ref, sh_ref, sc_ref)
    z = _bdot(h_ext, inw_ref[...])
    valid = _row_valid(seq)

    def inv_count(row0, half):
        pos = tile_pos + row0 + lax.broadcasted_iota(jnp.int32, (HALO, 1), 0)
        cnt = jnp.minimum(pos + half, seq) - jnp.maximum(pos - half, 0)
        return 1.0 / cnt.astype(F32)

    def pooled_block(r0, a_ext):
        if r0 == 0:
            a_ext = jnp.concatenate([jnp.where(valid[:HALO], a_ext[:HALO], 0.0), a_ext[HALO:]], axis=0)
        if r0 + MIX_SUB == SEQ_TILE:
            a_ext = jnp.concatenate([a_ext[:HALO + MIX_SUB],
                                     jnp.where(valid[HALO + SEQ_TILE:], a_ext[HALO + MIX_SUB:], 0.0)], axis=0)
        a_bf = a_ext.astype(BF16)
        starts = range(0, MIX_SUB, CHUNK)
        groups = []
        for grp, window in enumerate(POOL_WINDOWS):
            half = window // 2
            lanes = slice(grp * POOL_GROUP_DIM, (grp + 1) * POOL_GROUP_DIM)
            windows = jnp.concatenate([a_bf[t0:t0 + CHUNK + 2 * HALO, lanes] for t0 in starts], axis=1)
            totals = _bdot(band_ref[grp], windows)
            rows = []
            for c, t0 in enumerate(starts):
                total = totals[:, c * POOL_GROUP_DIM:(c + 1) * POOL_GROUP_DIM]
                centre = a_ext[HALO + t0:HALO + t0 + CHUNK, lanes]
                pooled = total * (1.0 / window) - centre
                if r0 + t0 == 0:
                    head = total[:HALO] * inv_count(0, half) - centre[:HALO]
                    pooled = jnp.concatenate([head, pooled[HALO:]], axis=0)
                if r0 + t0 + CHUNK == SEQ_TILE:
                    tail = total[CHUNK - HALO:] * inv_count(SEQ_TILE - HALO, half) - centre[CHUNK - HALO:]
                    pooled = jnp.concatenate([pooled[:CHUNK - HALO], tail], axis=0)
                rows.append(pooled)
            groups.append(jnp.concatenate(rows, axis=0))
        return jnp.concatenate(groups, axis=1)

    def gate_block(vn):
        n_chunks = MIX_SUB // CHUNK
        mixed = []
        for hd in range(SGU_HEADS):
            lanes = slice(hd * SGU_HEAD_DIM, (hd + 1) * SGU_HEAD_DIM)
            blocks = jnp.concatenate([vn[c * CHUNK:(c + 1) * CHUNK, lanes] for c in range(n_chunks)], axis=1)
            mixed.append(_bdot(sguw_ref[hd], blocks))
        return jnp.concatenate(
            [jnp.concatenate([m[:, c * SGU_HEAD_DIM:(c + 1) * SGU_HEAD_DIM] for m in mixed], axis=1) + sbias_ref[...]
             for c in range(n_chunks)], axis=0)

    for r0 in range(0, SEQ_TILE, MIX_SUB):
        pooled = pooled_block(r0, z[r0:r0 + MIX_SUB + 2 * HALO, :POOL_WIDTH])
        uv = z[HALO + r0:HALO + r0 + MIX_SUB, POOL_WIDTH:]
        uv = 0.5 * uv * (1.0 + lax.erf(uv * (0.5 ** 0.5)))
        u, v = uv[:, :SGU_WIDTH], uv[:, SGU_WIDTH:]
        gated = u * gate_block(_layer_norm(v, sg_ref[...], sb_ref[...]).astype(BF16))
        y = _bdot(jnp.concatenate([pooled, gated], axis=1).astype(BF16), outw_ref[...])
        r = ALPHA * x_ref[0, pl.ds(r0, MIX_SUB), :] + gt_ref[0] * y
        o_ref[0, pl.ds(r0, MIX_SUB), :] = _layer_norm(r, g_ref[...], b_ref[...])


def _pool_bands():
    p = lax.broadcasted_iota(jnp.int32, (CHUNK, CHUNK + 2 * HALO), 0) + HALO
    j = lax.broadcasted_iota(jnp.int32, (CHUNK, CHUNK + 2 * HALO), 1)
    return jnp.stack([((j >= p - w // 2) & (j < p + w // 2)).astype(BF16) for w in POOL_WINDOWS])


def _fold_body(pw_ref, ps_ref, ow_ref, o_ref):
    g = pl.program_id(0)

    @pl.when(g < N_POOL_GROUPS)
    def _():
        o_ref[...] = jnp.dot(pw_ref[0] * ps_ref[0], ow_ref[...], preferred_element_type=F32,
                             precision=lax.Precision.HIGHEST).astype(BF16)

    @pl.when(g >= N_POOL_GROUPS)
    def _():
        o_ref[...] = ow_ref[...].astype(BF16)


def _fold_pool_into_out(pool_w, pool_scale, out_w):
    rows, d = out_w.shape
    last = N_POOL_GROUPS - 1
    return pl.pallas_call(
        _fold_body,
        grid=(rows // POOL_GROUP_DIM,),
        in_specs=[pl.BlockSpec((1, POOL_GROUP_DIM, POOL_GROUP_DIM), lambda g: (jnp.minimum(g, last), 0, 0)),
                  pl.BlockSpec((1, 1, POOL_GROUP_DIM), lambda g: (jnp.minimum(g, last), 0, 0)),
                  pl.BlockSpec((POOL_GROUP_DIM, d), lambda g: (g, 0))],
        out_specs=pl.BlockSpec((POOL_GROUP_DIM, d), lambda g: (g, 0)),
        out_shape=jax.ShapeDtypeStruct((rows, d), BF16),
        name="fold_pool",
    )(pool_w, pool_scale.reshape(N_POOL_GROUPS, 1, POOL_GROUP_DIM), out_w)


def _mixer0(x, mods, row_off, in_w, sg, sb, sgu_w, sgu_bias, out_w, g, b):
    bsz, seq, d = x.shape
    prev, main, nxt = _x_specs(seq)
    band = _pool_bands()
    return pl.pallas_call(
        functools.partial(_mix0_body, seq),
        grid=(bsz, seq // SEQ_TILE),
        in_specs=[prev, main, nxt, _mod_spec(row_off), _mod_spec(row_off), _mod_spec(row_off),
                  _const_spec(in_w.shape),
                  _const_spec(band.shape), _const_spec(sg.shape), _const_spec(sb.shape), _const_spec(sgu_w.shape),
                  _const_spec(sgu_bias.shape), _const_spec(out_w.shape),
                  _const_spec(g.shape), _const_spec(b.shape)],
        out_specs=pl.BlockSpec((1, SEQ_TILE, d), lambda bb, i: (bb, i, 0)),
        out_shape=jax.ShapeDtypeStruct(x.shape, F32),
        compiler_params=pltpu.CompilerParams(vmem_limit_bytes=VMEM_LIMIT_BYTES),
        name="mixer0",
    )(x, x, x, mods[0], mods[1], mods[2], in_w, band, sg, sb, sgu_w, sgu_bias, out_w, g, b)


def _mix1_body(seq, xp_ref, x_ref, xn_ref, sh_ref, sc_ref, gt_ref, pw1_ref, pb1_ref, dww_ref, dwb_ref,
               cg_ref, cb_ref, pw2_ref, pb2_ref, g_ref, b_ref, o_ref, g_s, ge_s, go_s, d_s):
    h_ext = _modulated_ext(xp_ref, x_ref, xn_ref, sh_ref, sc_ref)
    ext = SEQ_TILE + 2 * HALO
    valid = _row_valid(seq)

    for blk in range(CONV_WIDTH // GLU_COLS):
        vcols = slice(blk * GLU_COLS, (blk + 1) * GLU_COLS)
        gcols = slice(CONV_WIDTH + blk * GLU_COLS, CONV_WIDTH + (blk + 1) * GLU_COLS)
        value = _bdot(h_ext, pw1_ref[:, vcols]) + pb1_ref[:, vcols]
        gate = _bdot(h_ext, pw1_ref[:, gcols]) + pb1_ref[:, gcols]
        glu = value * _sigmoid(gate)
        glu = jnp.concatenate([jnp.where(valid[:HALO], glu[:HALO], 0.0), glu[HALO:HALO + SEQ_TILE],
                               jnp.where(valid[HALO + SEQ_TILE:], glu[HALO + SEQ_TILE:], 0.0)], axis=0)
        for i in range(ext // SUBLANES):
            for jj in range(GLU_COLS // LANES):
                j = blk * (GLU_COLS // LANES) + jj
                g_s[pl.ds(i * SUBLANES * LANE_BLOCKS + j, SUBLANES, stride=LANE_BLOCKS), :] = (
                    glu[i * SUBLANES:(i + 1) * SUBLANES, jj * LANES:(jj + 1) * LANES])

    pair_rows = 2 * LANE_BLOCKS
    gt = g_s[...]
    ge_s[...] = gt.reshape(ext // 2, pair_rows, LANES).astype(BF16)
    go_s[pl.ds(0, ext // 2 - 1)] = (
        gt[LANE_BLOCKS:ext * LANE_BLOCKS - LANE_BLOCKS].reshape(ext // 2 - 1, pair_rows, LANES).astype(BF16))

    first = HALO - CONV_TAPS // 2
    n_tiles = CONV_PAIRS + (first + CONV_TAPS - 1) // 2

    def conv_step(i, carry):
        q0 = i * CONV_PAIRS
        te = [ge_s[q0 + n] for n in range(n_tiles)]
        to = [go_s[q0 + n] for n in range(n_tiles)]
        accs = [None] * CONV_PAIRS
        for k in range(CONV_TAPS):
            off = first + k
            wk = dww_ref[k].astype(F32)
            for q in range(CONV_PAIRS):
                tile = (te if off % 2 == 0 else to)[q + off // 2]
                prod = wk * tile.astype(F32)
                accs[q] = prod if accs[q] is None else accs[q] + prod
        for q in range(CONV_PAIRS):
            row = pl.multiple_of((q0 + q) * pair_rows, pair_rows)
            d_s[pl.ds(row, pair_rows), :] = accs[q] + dwb_ref[...]
        return carry

    lax.fori_loop(0, SEQ_TILE // (2 * CONV_PAIRS), conv_step, 0)

    for r0 in range(0, SEQ_TILE, MIX_SUB):
        d = jnp.concatenate(
            [jnp.concatenate([d_s[pl.ds(i * SUBLANES * LANE_BLOCKS + j, SUBLANES, stride=LANE_BLOCKS), :]
                              for j in range(LANE_BLOCKS)], axis=1)
             for i in range(r0 // SUBLANES, (r0 + MIX_SUB) // SUBLANES)], axis=0)
        dn = _layer_norm(d, cg_ref[...], cb_ref[...])
        dn = (dn * _sigmoid(dn)).astype(BF16)
        y = _bdot(dn, pw2_ref[...]) + pb2_ref[...]
        r = ALPHA * x_ref[0, pl.ds(r0, MIX_SUB), :] + gt_ref[0] * y
        o_ref[0, pl.ds(r0, MIX_SUB), :] = _layer_norm(r, g_ref[...], b_ref[...])


def _mixer1(x, mods, row_off, pw1_w, pw1_b, dw_w, dw_b, cg, cb, pw2_w, pw2_b, g, b):
    bsz, seq, d = x.shape
    prev, main, nxt = _x_specs(seq)
    return pl.pallas_call(
        functools.partial(_mix1_body, seq),
        grid=(bsz, seq // SEQ_TILE),
        in_specs=[prev, main, nxt, _mod_spec(row_off), _mod_spec(row_off), _mod_spec(row_off),
                  _const_spec(pw1_w.shape), _const_spec(pw1_b.shape), _const_spec(dw_w.shape),
                  _const_spec(dw_b.shape), _const_spec(cg.shape), _const_spec(cb.shape),
                  _const_spec(pw2_w.shape), _const_spec(pw2_b.shape),
                  _const_spec(g.shape), _const_spec(b.shape)],
        out_specs=pl.BlockSpec((1, SEQ_TILE, d), lambda bb, i: (bb, i, 0)),
        out_shape=jax.ShapeDtypeStruct(x.shape, F32),
        scratch_shapes=[pltpu.VMEM(((SEQ_TILE + 2 * HALO) * LANE_BLOCKS, LANES), F32),
                        pltpu.VMEM(((SEQ_TILE + 2 * HALO) // 2, 2 * LANE_BLOCKS, LANES), BF16),
                        pltpu.VMEM(((SEQ_TILE + 2 * HALO) // 2, 2 * LANE_BLOCKS, LANES), BF16),
                        pltpu.VMEM((SEQ_TILE * LANE_BLOCKS, LANES), F32)],
        compiler_params=pltpu.CompilerParams(vmem_limit_bytes=VMEM_LIMIT_BYTES),
        name="mixer1",
    )(x, x, x, mods[0], mods[1], mods[2], pw1_w, pw1_b, dw_w, dw_b, cg, cb, pw2_w, pw2_b, g, b)


def _mlp_body(x_ref, sh_ref, sc_ref, gt_ref, w1_ref, w2_ref, g_ref, b_ref, o_ref):
    for s in range(MLP_TILE // MLP_SUB):
        rows = pl.ds(s * MLP_SUB, MLP_SUB)
        x = x_ref[0, rows, :]
        h = (x * (1.0 + sc_ref[0]) + sh_ref[0]).astype(BF16)
        acc = None
        for f in range(D_FF // FF_CHUNK):
            cols = slice(f * FF_CHUNK, (f + 1) * FF_CHUNK)
            u = jnp.maximum(_bdot(h, w1_ref[:, cols]), 0.0)
            part = _bdot((u * u).astype(BF16), w2_ref[cols, :])
            acc = part if acc is None else acc + part
        r = ALPHA * x + gt_ref[0] * acc
        o_ref[0, rows, :] = _layer_norm(r, g_ref[...], b_ref[...])


def _mlp(x, mods, row_off, w1, w2, g, b):
    bsz, seq, d = x.shape
    tile = pl.BlockSpec((1, MLP_TILE, d), lambda bb, i: (bb, i, 0))
    return pl.pallas_call(
        _mlp_body,
        grid=(bsz, seq // MLP_TILE),
        in_specs=[tile, _mod_spec(row_off), _mod_spec(row_off), _mod_spec(row_off),
                  _const_spec(w1.shape), _const_spec(w2.shape), _const_spec(g.shape), _const_spec(b.shape)],
        out_specs=tile,
        out_shape=jax.ShapeDtypeStruct(x.shape, F32),
        compiler_params=pltpu.CompilerParams(vmem_limit_bytes=VMEM_LIMIT_BYTES),
        name="mlp",
    )(x, mods[3], mods[4], mods[5], w1, w2, g, b)


def _row(v):
    return v.reshape(1, -1)


def kernel(x_prompt, x_sample, c_prompt, c_sample, l0_ada_w, l0_ada_b, l0_in_w, l0_pool_w, l0_pool_scale, l0_sgu_ln_g, l0_sgu_ln_b, l0_sgu_w, l0_sgu_b, l0_out_w, l0_ln1_g, l0_ln1_b, l0_mlp_w1, l0_mlp_w2, l0_ln2_g, l0_ln2_b, l1_ada_w, l1_ada_b, l1_pw1_w, l1_pw1_b, l1_dw_w, l1_dw_b, l1_cnorm_g, l1_cnorm_b, l1_pw2_w, l1_pw2_b, l1_ln1_g, l1_ln1_b, l1_mlp_w1, l1_mlp_w2, l1_ln2_g, l1_ln2_b):
    n_prompt = c_prompt.shape[0]
    c_all = jnp.concatenate([c_prompt, c_sample], axis=0)

    def mods_of(ada_w, ada_b):
        mod = _adaln(c_all, ada_w, ada_b)
        return [m.reshape(-1, 1, D_MODEL) for m in jnp.split(mod, N_MOD, axis=-1)]

    mods0 = mods_of(l0_ada_w, l0_ada_b)
    mods1 = mods_of(l1_ada_w, l1_ada_b)

    sgu_bias = jnp.repeat(l0_sgu_b.T, SGU_HEAD_DIM, axis=1)

    in_w = l0_in_w.astype(BF16)
    sgu_w = l0_sgu_w.astype(BF16)
    out_w = _fold_pool_into_out(l0_pool_w, l0_pool_scale, l0_out_w)
    w1_0, w2_0 = l0_mlp_w1.astype(BF16), l0_mlp_w2.astype(BF16)
    pw1_w, pw2_w = l1_pw1_w.astype(BF16), l1_pw2_w.astype(BF16)
    dw_pair = jnp.tile(l1_dw_w.reshape(CONV_TAPS, 1, LANE_BLOCKS, LANES), (1, 2, 1, 1))
    dw_pair = dw_pair.reshape(CONV_TAPS, 2 * LANE_BLOCKS, LANES).astype(BF16)
    dwb_pair = jnp.tile(l1_dw_b.reshape(LANE_BLOCKS, LANES), (2, 1))
    w1_1, w2_1 = l1_mlp_w1.astype(BF16), l1_mlp_w2.astype(BF16)

    def trunk(x, row_off):
        x = _mixer0(x, mods0, row_off, in_w, _row(l0_sgu_ln_g), _row(l0_sgu_ln_b),
                    sgu_w, sgu_bias, out_w, _row(l0_ln1_g), _row(l0_ln1_b))
        x = _mlp(x, mods0, row_off, w1_0, w2_0, _row(l0_ln2_g), _row(l0_ln2_b))
        x = _mixer1(x, mods1, row_off, pw1_w, _row(l1_pw1_b), dw_pair, dwb_pair, _row(l1_cnorm_g),
                    _row(l1_cnorm_b), pw2_w, _row(l1_pw2_b), _row(l1_ln1_g), _row(l1_ln1_b))
        x = _mlp(x, mods1, row_off, w1_1, w2_1, _row(l1_ln2_g), _row(l1_ln2_b))
        return x

    return (trunk(x_prompt, 0), trunk(x_sample, n_prompt))
```

```python
import functools

import jax
import jax.numpy as jnp
from jax import lax
from jax.experimental import pallas as pl
from jax.experimental.pallas import tpu as pltpu

D_MODEL = 1024
DEPTH = 2
POOL_WINDOWS = (2, 4, 8, 16)
N_POOL_GROUPS = 4
POOL_GROUP_DIM = 128
POOL_WIDTH = 512
SGU_HEADS = 4
SGU_HEAD_DIM = 128
SGU_WIDTH = 512
CHUNK = 128
IN_WIDTH = POOL_WIDTH + 2 * SGU_WIDTH
CONV_WIDTH = D_MODEL
CONV_TAPS = 31
D_FF = 4 * D_MODEL
N_MOD = 6
ALPHA = (2.0 * DEPTH) ** 0.25
LN_EPS = 1e-5

F32 = jnp.float32
BF16 = jnp.bfloat16

V7X_VMEM_BYTES = 64 * 1024 * 1024
VMEM_LIMIT_BYTES = V7X_VMEM_BYTES - 8 * 1024 * 1024

SEQ_TILE = 1024
HALO = 16
FF_CHUNK = 1024
MLP_TILE = 1024
MLP_SUB = 256
MIX_SUB = 256
SUBLANES = 8
LANES = 128
LANE_BLOCKS = CONV_WIDTH // LANES
assert LANE_BLOCKS == SUBLANES
CONV_PAIRS = 16
GLU_COLS = 256
ADA_COLS = 1536


def _layer_norm(r, g, b):
    mu = jnp.mean(r, axis=-1, keepdims=True)
    d = r - mu
    var = jnp.mean(d * d, axis=-1, keepdims=True)
    return d * lax.rsqrt(var + LN_EPS) * g + b


def _sigmoid(x):
    return 1.0 / (1.0 + jnp.exp(-x))


def _bdot(a, b):
    return jnp.dot(a, b, preferred_element_type=F32)


def _mlp_rows(x, shift, scale, gate, w1_ref, w2_ref, g, b):
    h = (x * (1.0 + scale) + shift).astype(BF16)
    acc = None
    for f in range(D_FF // FF_CHUNK):
        cols = slice(f * FF_CHUNK, (f + 1) * FF_CHUNK)
        u = jnp.maximum(_bdot(h, w1_ref[:, cols]), 0.0)
        part = _bdot((u * u).astype(BF16), w2_ref[cols, :])
        acc = part if acc is None else acc + part
    return _layer_norm(ALPHA * x + gate * acc, g, b)


def _ada_body(c_ref, w_ref, b_ref, o_ref):
    c = c_ref[...]
    s = (c * _sigmoid(c)).astype(BF16)
    o_ref[...] = _bdot(s, w_ref[...].astype(BF16)) + b_ref[...]


def _adaln(c, w, b):
    m, d = c.shape
    n = w.shape[1]
    return pl.pallas_call(
        _ada_body,
        grid=(n // ADA_COLS,),
        in_specs=[
            pl.BlockSpec((m, d), lambda j: (0, 0)),
            pl.BlockSpec((d, ADA_COLS), lambda j: (0, j)),
            pl.BlockSpec((1, ADA_COLS), lambda j: (0, j)),
        ],
        out_specs=pl.BlockSpec((m, ADA_COLS), lambda j: (0, j)),
        out_shape=jax.ShapeDtypeStruct((m, n), F32),
        compiler_params=pltpu.CompilerParams(vmem_limit_bytes=VMEM_LIMIT_BYTES),
        name="adaln",
    )(c, w, b.reshape(1, n))


def _const_spec(shape):
    nd = len(shape)
    return pl.BlockSpec(shape, lambda b, i: (0,) * nd, pipeline_mode=pl.Buffered(1))


def _mod_spec(row_off):
    return pl.BlockSpec((1, 1, D_MODEL), lambda b, i: (b + row_off, 0, 0))


def _x_specs(seq):
    per_tile = SEQ_TILE // HALO
    last = seq // HALO - 1
    prev = pl.BlockSpec((1, HALO, D_MODEL), lambda b, i: (b, jnp.maximum(i * per_tile - 1, 0), 0))
    main = pl.BlockSpec((1, SEQ_TILE, D_MODEL), lambda b, i: (b, i, 0))
    nxt = pl.BlockSpec((1, HALO, D_MODEL), lambda b, i: (b, jnp.minimum((i + 1) * per_tile, last), 0))
    return prev, main, nxt


def _modulated_ext(xp_ref, x_ref, xn_ref, sh_ref, sc_ref):
    x_ext = jnp.concatenate([xp_ref[0], x_ref[0], xn_ref[0]], axis=0)
    return (x_ext * (1.0 + sc_ref[0]) + sh_ref[0]).astype(BF16)


def _row_valid(seq):
    i = pl.program_id(1)
    pos = i * SEQ_TILE - HALO + lax.broadcasted_iota(jnp.int32, (SEQ_TILE + 2 * HALO, 1), 0)
    return (pos >= 0) & (pos < seq)


def _layer0_body(seq, xp_ref, x_ref, xn_ref, sh_ref, sc_ref, gt_ref, shf_ref, scf_ref, gtf_ref, inw_ref,
                 band_ref, sg_ref, sb_ref, sguw_ref, sbias_ref, outw_ref, g_ref, b_ref,
                 w1_ref, w2_ref, g2_ref, b2_ref, o_ref):
    tile_pos = pl.program_id(1) * SEQ_TILE
    h_ext = _modulated_ext(xp_ref, x_ref, xn_ref, sh_ref, sc_ref)
    z = _bdot(h_ext, inw_ref[...])
    valid = _row_valid(seq)

    def inv_count(row0, half):
        pos = tile_pos + row0 + lax.broadcasted_iota(jnp.int32, (HALO, 1), 0)
        cnt = jnp.minimum(pos + half, seq) - jnp.maximum(pos - half, 0)
        return 1.0 / cnt.astype(F32)

    def pooled_block(r0, a_ext):
        if r0 == 0:
            a_ext = jnp.concatenate([jnp.where(valid[:HALO], a_ext[:HALO], 0.0), a_ext[HALO:]], axis=0)
        if r0 + MIX_SUB == SEQ_TILE:
            a_ext = jnp.concatenate([a_ext[:HALO + MIX_SUB],
                                     jnp.where(valid[HALO + SEQ_TILE:], a_ext[HALO + MIX_SUB:], 0.0)], axis=0)
        a_bf = a_ext.astype(BF16)
        starts = range(0, MIX_SUB, CHUNK)
        groups = []
        for grp, window in enumerate(POOL_WINDOWS):
            half = window // 2
            lanes = slice(grp * POOL_GROUP_DIM, (grp + 1) * POOL_GROUP_DIM)
            windows = jnp.concatenate([a_bf[t0:t0 + CHUNK + 2 * HALO, lanes] for t0 in starts], axis=1)
            totals = _bdot(band_ref[grp], windows)
            rows = []
            for c, t0 in enumerate(starts):
                total = totals[:, c * POOL_GROUP_DIM:(c + 1) * POOL_GROUP_DIM]
                centre = a_ext[HALO + t0:HALO + t0 + CHUNK, lanes]
                pooled = total * (1.0 / window) - centre
                if r0 + t0 == 0:
                    head = total[:HALO] * inv_count(0, half) - centre[:HALO]
                    pooled = jnp.concatenate([head, pooled[HALO:]], axis=0)
                if r0 + t0 + CHUNK == SEQ_TILE:
                    tail = total[CHUNK - HALO:] * inv_count(SEQ_TILE - HALO, half) - centre[CHUNK - HALO:]
                    pooled = jnp.concatenate([pooled[:CHUNK - HALO], tail], axis=0)
                rows.append(pooled)
            groups.append(jnp.concatenate(rows, axis=0))
        return jnp.concatenate(groups, axis=1)

    def gate_block(vn):
        n_chunks = MIX_SUB // CHUNK
        mixed = []
        for hd in range(SGU_HEADS):
            lanes = slice(hd * SGU_HEAD_DIM, (hd + 1) * SGU_HEAD_DIM)
            blocks = jnp.concatenate([vn[c * CHUNK:(c + 1) * CHUNK, lanes] for c in range(n_chunks)], axis=1)
            mixed.append(_bdot(sguw_ref[hd], blocks))
        return jnp.concatenate(
            [jnp.concatenate([m[:, c * SGU_HEAD_DIM:(c + 1) * SGU_HEAD_DIM] for m in mixed], axis=1) + sbias_ref[...]
             for c in range(n_chunks)], axis=0)

    for r0 in range(0, SEQ_TILE, MIX_SUB):
        pooled = pooled_block(r0, z[r0:r0 + MIX_SUB + 2 * HALO, :POOL_WIDTH])
        uv = z[HALO + r0:HALO + r0 + MIX_SUB, POOL_WIDTH:]
        uv = 0.5 * uv * (1.0 + lax.erf(uv * (0.5 ** 0.5)))
        u, v = uv[:, :SGU_WIDTH], uv[:, SGU_WIDTH:]
        gated = u * gate_block(_layer_norm(v, sg_ref[...], sb_ref[...]).astype(BF16))
        y = _bdot(jnp.concatenate([pooled, gated], axis=1).astype(BF16), outw_ref[...])
        r = ALPHA * x_ref[0, pl.ds(r0, MIX_SUB), :] + gt_ref[0] * y
        mixed = _layer_norm(r, g_ref[...], b_ref[...])
        o_ref[0, pl.ds(r0, MIX_SUB), :] = _mlp_rows(mixed, shf_ref[0], scf_ref[0], gtf_ref[0], w1_ref, w2_ref,
                                                    g2_ref[...], b2_ref[...])


def _pool_bands():
    p = lax.broadcasted_iota(jnp.int32, (CHUNK, CHUNK + 2 * HALO), 0) + HALO
    j = lax.broadcasted_iota(jnp.int32, (CHUNK, CHUNK + 2 * HALO), 1)
    return jnp.stack([((j >= p - w // 2) & (j < p + w // 2)).astype(BF16) for w in POOL_WINDOWS])


def _fold_body(pw_ref, ps_ref, ow_ref, o_ref):
    g = pl.program_id(0)

    @pl.when(g < N_POOL_GROUPS)
    def _():
        o_ref[...] = jnp.dot(pw_ref[0] * ps_ref[0], ow_ref[...], preferred_element_type=F32,
                             precision=lax.Precision.HIGHEST).astype(BF16)

    @pl.when(g >= N_POOL_GROUPS)
    def _():
        o_ref[...] = ow_ref[...].astype(BF16)


def _fold_pool_into_out(pool_w, pool_scale, out_w):
    rows, d = out_w.shape
    last = N_POOL_GROUPS - 1
    return pl.pallas_call(
        _fold_body,
        grid=(rows // POOL_GROUP_DIM,),
        in_specs=[pl.BlockSpec((1, POOL_GROUP_DIM, POOL_GROUP_DIM), lambda g: (jnp.minimum(g, last), 0, 0)),
                  pl.BlockSpec((1, 1, POOL_GROUP_DIM), lambda g: (jnp.minimum(g, last), 0, 0)),
                  pl.BlockSpec((POOL_GROUP_DIM, d), lambda g: (g, 0))],
        out_specs=pl.BlockSpec((POOL_GROUP_DIM, d), lambda g: (g, 0)),
        out_shape=jax.ShapeDtypeStruct((rows, d), BF16),
        name="fold_pool",
    )(pool_w, pool_scale.reshape(N_POOL_GROUPS, 1, POOL_GROUP_DIM), out_w)


def _layer0(x, mods, row_off, in_w, sg, sb, sgu_w, sgu_bias, out_w, g, b, w1, w2, g2, b2):
    bsz, seq, d = x.shape
    prev, main, nxt = _x_specs(seq)
    band = _pool_bands()
    consts = (in_w, band, sg, sb, sgu_w, sgu_bias, out_w, g, b, w1, w2, g2, b2)
    return pl.pallas_call(
        functools.partial(_layer0_body, seq),
        grid=(bsz, seq // SEQ_TILE),
        in_specs=[prev, main, nxt] + [_mod_spec(row_off)] * N_MOD + [_const_spec(c.shape) for c in consts],
        out_specs=pl.BlockSpec((1, SEQ_TILE, d), lambda bb, i: (bb, i, 0)),
        out_shape=jax.ShapeDtypeStruct(x.shape, F32),
        compiler_params=pltpu.CompilerParams(vmem_limit_bytes=VMEM_LIMIT_BYTES),
        name="layer0",
    )(x, x, x, *mods, *consts)


def _mix1_body(seq, xp_ref, x_ref, xn_ref, sh_ref, sc_ref, gt_ref, pw1_ref, pb1_ref, dww_ref, dwb_ref,
               cg_ref, cb_ref, pw2_ref, pb2_ref, g_ref, b_ref, o_ref, g_s, ge_s, go_s, d_s):
    h_ext = _modulated_ext(xp_ref, x_ref, xn_ref, sh_ref, sc_ref)
    ext = SEQ_TILE + 2 * HALO
    valid = _row_valid(seq)

    for blk in range(CONV_WIDTH // GLU_COLS):
        vcols = slice(blk * GLU_COLS, (blk + 1) * GLU_COLS)
        gcols = slice(CONV_WIDTH + blk * GLU_COLS, CONV_WIDTH + (blk + 1) * GLU_COLS)
        value = _bdot(h_ext, pw1_ref[:, vcols]) + pb1_ref[:, vcols]
        gate = _bdot(h_ext, pw1_ref[:, gcols]) + pb1_ref[:, gcols]
        glu = value * _sigmoid(gate)
        glu = jnp.concatenate([jnp.where(valid[:HALO], glu[:HALO], 0.0), glu[HALO:HALO + SEQ_TILE],
                               jnp.where(valid[HALO + SEQ_TILE:], glu[HALO + SEQ_TILE:], 0.0)], axis=0)
        for i in range(ext // SUBLANES):
            for jj in range(GLU_COLS // LANES):
                j = blk * (GLU_COLS // LANES) + jj
                g_s[pl.ds(i * SUBLANES * LANE_BLOCKS + j, SUBLANES, stride=LANE_BLOCKS), :] = (
                    glu[i * SUBLANES:(i + 1) * SUBLANES, jj * LANES:(jj + 1) * LANES])

    pair_rows = 2 * LANE_BLOCKS
    gt = g_s[...]
    ge_s[...] = gt.reshape(ext // 2, pair_rows, LANES).astype(BF16)
    go_s[pl.ds(0, ext // 2 - 1)] = (
        gt[LANE_BLOCKS:ext * LANE_BLOCKS - LANE_BLOCKS].reshape(ext // 2 - 1, pair_rows, LANES).astype(BF16))

    first = HALO - CONV_TAPS // 2
    n_tiles = CONV_PAIRS + (first + CONV_TAPS - 1) // 2

    def conv_step(i, carry):
        q0 = i * CONV_PAIRS
        te = [ge_s[q0 + n] for n in range(n_tiles)]
        to = [go_s[q0 + n] for n in range(n_tiles)]
        accs = [None] * CONV_PAIRS
        for k in range(CONV_TAPS):
            off = first + k
            wk = dww_ref[k].astype(F32)
            for q in range(CONV_PAIRS):
                tile = (te if off % 2 == 0 else to)[q + off // 2]
                prod = wk * tile.astype(F32)
                accs[q] = prod if accs[q] is None else accs[q] + prod
        for q in range(CONV_PAIRS):
            row = pl.multiple_of((q0 + q) * pair_rows, pair_rows)
            d_s[pl.ds(row, pair_rows), :] = accs[q] + dwb_ref[...]
        return carry

    lax.fori_loop(0, SEQ_TILE // (2 * CONV_PAIRS), conv_step, 0)

    for r0 in range(0, SEQ_TILE, MIX_SUB):
        d = jnp.concatenate(
            [jnp.concatenate([d_s[pl.ds(i * SUBLANES * LANE_BLOCKS + j, SUBLANES, stride=LANE_BLOCKS), :]
                              for j in range(LANE_BLOCKS)], axis=1)
             for i in range(r0 // SUBLANES, (r0 + MIX_SUB) // SUBLANES)], axis=0)
        dn = _layer_norm(d, cg_ref[...], cb_ref[...])
        dn = (dn * _sigmoid(dn)).astype(BF16)
        y = _bdot(dn, pw2_ref[...]) + pb2_ref[...]
        r = ALPHA * x_ref[0, pl.ds(r0, MIX_SUB), :] + gt_ref[0] * y
        o_ref[0, pl.ds(r0, MIX_SUB), :] = _layer_norm(r, g_ref[...], b_ref[...])


def _mixer1(x, mods, row_off, pw1_w, pw1_b, dw_w, dw_b, cg, cb, pw2_w, pw2_b, g, b):
    bsz, seq, d = x.shape
    prev, main, nxt = _x_specs(seq)
    return pl.pallas_call(
        functools.partial(_mix1_body, seq),
        grid=(bsz, seq // SEQ_TILE),
        in_specs=[prev, main, nxt, _mod_spec(row_off), _mod_spec(row_off), _mod_spec(row_off),
                  _const_spec(pw1_w.shape), _const_spec(pw1_b.shape), _const_spec(dw_w.shape),
                  _const_spec(dw_b.shape), _const_spec(cg.shape), _const_spec(cb.shape),
                  _const_spec(pw2_w.shape), _const_spec(pw2_b.shape),
                  _const_spec(g.shape), _const_spec(b.shape)],
        out_specs=pl.BlockSpec((1, SEQ_TILE, d), lambda bb, i: (bb, i, 0)),
        out_shape=jax.ShapeDtypeStruct(x.shape, F32),
        scratch_shapes=[pltpu.VMEM(((SEQ_TILE + 2 * HALO) * LANE_BLOCKS, LANES), F32),
                        pltpu.VMEM(((SEQ_TILE + 2 * HALO) // 2, 2 * LANE_BLOCKS, LANES), BF16),
                        pltpu.VMEM(((SEQ_TILE + 2 * HALO) // 2, 2 * LANE_BLOCKS, LANES), BF16),
                        pltpu.VMEM((SEQ_TILE * LANE_BLOCKS, LANES), F32)],
        compiler_params=pltpu.CompilerParams(vmem_limit_bytes=VMEM_LIMIT_BYTES),
        name="mixer1",
    )(x, x, x, mods[0], mods[1], mods[2], pw1_w, pw1_b, dw_w, dw_b, cg, cb, pw2_w, pw2_b, g, b)


def _mlp_body(x_ref, sh_ref, sc_ref, gt_ref, w1_ref, w2_ref, g_ref, b_ref, o_ref):
    for s in range(MLP_TILE // MLP_SUB):
        rows = pl.ds(s * MLP_SUB, MLP_SUB)
        o_ref[0, rows, :] = _mlp_rows(x_ref[0, rows, :], sh_ref[0], sc_ref[0], gt_ref[0], w1_ref, w2_ref,
                                      g_ref[...], b_ref[...])


def _mlp(x, mods, row_off, w1, w2, g, b):
    bsz, seq, d = x.shape
    tile = pl.BlockSpec((1, MLP_TILE, d), lambda bb, i: (bb, i, 0))
    return pl.pallas_call(
        _mlp_body,
        grid=(bsz, seq // MLP_TILE),
        in_specs=[tile, _mod_spec(row_off), _mod_spec(row_off), _mod_spec(row_off),
                  _const_spec(w1.shape), _const_spec(w2.shape), _const_spec(g.shape), _const_spec(b.shape)],
        out_specs=tile,
        out_shape=jax.ShapeDtypeStruct(x.shape, F32),
        compiler_params=pltpu.CompilerParams(vmem_limit_bytes=VMEM_LIMIT_BYTES),
        name="mlp",
    )(x, mods[3], mods[4], mods[5], w1, w2, g, b)


def _row(v):
    return v.reshape(1, -1)


def kernel(x_prompt, x_sample, c_prompt, c_sample, l0_ada_w, l0_ada_b, l0_in_w, l0_pool_w, l0_pool_scale, l0_sgu_ln_g, l0_sgu_ln_b, l0_sgu_w, l0_sgu_b, l0_out_w, l0_ln1_g, l0_ln1_b, l0_mlp_w1, l0_mlp_w2, l0_ln2_g, l0_ln2_b, l1_ada_w, l1_ada_b, l1_pw1_w, l1_pw1_b, l1_dw_w, l1_dw_b, l1_cnorm_g, l1_cnorm_b, l1_pw2_w, l1_pw2_b, l1_ln1_g, l1_ln1_b, l1_mlp_w1, l1_mlp_w2, l1_ln2_g, l1_ln2_b):
    n_prompt = c_prompt.shape[0]
    c_all = jnp.concatenate([c_prompt, c_sample], axis=0)

    def mods_of(ada_w, ada_b):
        mod = _adaln(c_all, ada_w, ada_b)
        return [m.reshape(-1, 1, D_MODEL) for m in jnp.split(mod, N_MOD, axis=-1)]

    mods0 = mods_of(l0_ada_w, l0_ada_b)
    mods1 = mods_of(l1_ada_w, l1_ada_b)

    sgu_bias = jnp.repeat(l0_sgu_b.T, SGU_HEAD_DIM, axis=1)

    in_w = l0_in_w.astype(BF16)
    sgu_w = l0_sgu_w.astype(BF16)
    out_w = _fold_pool_into_out(l0_pool_w, l0_pool_scale, l0_out_w)
    w1_0, w2_0 = l0_mlp_w1.astype(BF16), l0_mlp_w2.astype(BF16)
    pw1_w, pw2_w = l1_pw1_w.astype(BF16), l1_pw2_w.astype(BF16)
    dw_pair = jnp.tile(l1_dw_w.reshape(CONV_TAPS, 1, LANE_BLOCKS, LANES), (1, 2, 1, 1))
    dw_pair = dw_pair.reshape(CONV_TAPS, 2 * LANE_BLOCKS, LANES).astype(BF16)
    dwb_pair = jnp.tile(l1_dw_b.reshape(LANE_BLOCKS, LANES), (2, 1))
    w1_1, w2_1 = l1_mlp_w1.astype(BF16), l1_mlp_w2.astype(BF16)

    def trunk(x, row_off):
        x = _layer0(x, mods0, row_off, in_w, _row(l0_sgu_ln_g), _row(l0_sgu_ln_b),
                    sgu_w, sgu_bias, out_w, _row(l0_ln1_g), _row(l0_ln1_b),
                    w1_0, w2_0, _row(l0_ln2_g), _row(l0_ln2_b))
        x = _mixer1(x, mods1, row_off, pw1_w, _row(l1_pw1_b), dw_pair, dwb_pair, _row(l1_cnorm_g),
                    _row(l1_cnorm_b), pw2_w, _row(l1_pw2_b), _row(l1_ln1_g), _row(l1_ln1_b))
        x = _mlp(x, mods1, row_off, w1_1, w2_1, _row(l1_ln2_g), _row(l1_ln2_b))
        return x

    return (trunk(x_prompt, 0), trunk(x_sample, n_prompt))
```

```python
import functools

import jax
import jax.numpy as jnp
from jax import lax
from jax.experimental import pallas as pl
from jax.experimental.pallas import tpu as pltpu

D_MODEL = 1024
DEPTH = 2
POOL_WINDOWS = (2, 4, 8, 16)
N_POOL_GROUPS = 4
POOL_GROUP_DIM = 128
POOL_WIDTH = 512
SGU_HEADS = 4
SGU_HEAD_DIM = 128
SGU_WIDTH = 512
CHUNK = 128
IN_WIDTH = POOL_WIDTH + 2 * SGU_WIDTH
CONV_WIDTH = D_MODEL
CONV_TAPS = 31
D_FF = 4 * D_MODEL
N_MOD = 6
ALPHA = (2.0 * DEPTH) ** 0.25
LN_EPS = 1e-5

F32 = jnp.float32
BF16 = jnp.bfloat16

V7X_VMEM_BYTES = 64 * 1024 * 1024
VMEM_LIMIT_BYTES = V7X_VMEM_BYTES - 8 * 1024 * 1024

SEQ_TILE = 1024
HALO = 16
FF_CHUNK = 1024
MLP_TILE = 1024
MLP_SUB = 256
MIX_SUB = 256
SUBLANES = 8
LANES = 128
LANE_BLOCKS = CONV_WIDTH // LANES
assert LANE_BLOCKS == SUBLANES
CONV_PAIRS = 16
GLU_COLS = 256
ADA_COLS = 1536


def _layer_norm(r, g, b):
    mu = jnp.mean(r, axis=-1, keepdims=True)
    d = r - mu
    var = jnp.mean(d * d, axis=-1, keepdims=True)
    return d * lax.rsqrt(var + LN_EPS) * g + b


def _sigmoid(x):
    return 1.0 / (1.0 + jnp.exp(-x))


def _bdot(a, b):
    return jnp.dot(a, b, preferred_element_type=F32)


def _ada_body(c_ref, w_ref, b_ref, o_ref):
    c = c_ref[...]
    s = (c * _sigmoid(c)).astype(BF16)
    o_ref[...] = _bdot(s, w_ref[...].astype(BF16)) + b_ref[...]


def _adaln(c, w, b):
    m, d = c.shape
    n = w.shape[1]
    return pl.pallas_call(
        _ada_body,
        grid=(n // ADA_COLS,),
        in_specs=[
            pl.BlockSpec((m, d), lambda j: (0, 0)),
            pl.BlockSpec((d, ADA_COLS), lambda j: (0, j)),
            pl.BlockSpec((1, ADA_COLS), lambda j: (0, j)),
        ],
        out_specs=pl.BlockSpec((m, ADA_COLS), lambda j: (0, j)),
        out_shape=jax.ShapeDtypeStruct((m, n), F32),
        compiler_params=pltpu.CompilerParams(vmem_limit_bytes=VMEM_LIMIT_BYTES),
        name="adaln",
    )(c, w, b.reshape(1, n))


def _const_spec(shape):
    nd = len(shape)
    return pl.BlockSpec(shape, lambda b, i: (0,) * nd, pipeline_mode=pl.Buffered(1))


def _mod_spec(row_off):
    return pl.BlockSpec((1, 1, D_MODEL), lambda b, i: (b + row_off, 0, 0))


def _x_specs(seq):
    per_tile = SEQ_TILE // HALO
    last = seq // HALO - 1
    prev = pl.BlockSpec((1, HALO, D_MODEL), lambda b, i: (b, jnp.maximum(i * per_tile - 1, 0), 0))
    main = pl.BlockSpec((1, SEQ_TILE, D_MODEL), lambda b, i: (b, i, 0))
    nxt = pl.BlockSpec((1, HALO, D_MODEL), lambda b, i: (b, jnp.minimum((i + 1) * per_tile, last), 0))
    return prev, main, nxt


def _modulated_ext(xp_ref, x_ref, xn_ref, sh_ref, sc_ref):
    x_ext = jnp.concatenate([xp_ref[0], x_ref[0], xn_ref[0]], axis=0)
    return (x_ext * (1.0 + sc_ref[0]) + sh_ref[0]).astype(BF16)


def _row_valid(seq):
    i = pl.program_id(1)
    pos = i * SEQ_TILE - HALO + lax.broadcasted_iota(jnp.int32, (SEQ_TILE + 2 * HALO, 1), 0)
    return (pos >= 0) & (pos < seq)


def _mix0_body(seq, xp_ref, x_ref, xn_ref, sh_ref, sc_ref, gt_ref, inw_ref,
               band_ref, sg_ref, sb_ref, sguw_ref, sbias_ref, outw_ref, g_ref, b_ref, o_ref):
    tile_pos = pl.program_id(1) * SEQ_TILE
    h_ext = _modulated_ext(xp_ref, x_ref, xn_ref, sh_ref, sc_ref)
    z = _bdot(h_ext, inw_ref[...])
    valid = _row_valid(seq)

    def inv_count(row0, half):
        pos = tile_pos + row0 + lax.broadcasted_iota(jnp.int32, (HALO, 1), 0)
        cnt = jnp.minimum(pos + half, seq) - jnp.maximum(pos - half, 0)
        return 1.0 / cnt.astype(F32)

    def pooled_block(r0, a_ext):
        if r0 == 0:
            a_ext = jnp.concatenate([jnp.where(valid[:HALO], a_ext[:HALO], 0.0), a_ext[HALO:]], axis=0)
        if r0 + MIX_SUB == SEQ_TILE:
            a_ext = jnp.concatenate([a_ext[:HALO + MIX_SUB],
                                     jnp.where(valid[HALO + SEQ_TILE:], a_ext[HALO + MIX_SUB:], 0.0)], axis=0)
        a_bf = a_ext.astype(BF16)
        starts = range(0, MIX_SUB, CHUNK)
        groups = []
        for grp, window in enumerate(POOL_WINDOWS):
            half = window // 2
            lanes = slice(grp * POOL_GROUP_DIM, (grp + 1) * POOL_GROUP_DIM)
            windows = jnp.concatenate([a_bf[t0:t0 + CHUNK + 2 * HALO, lanes] for t0 in starts], axis=1)
            totals = _bdot(band_ref[grp], windows)
            rows = []
            for c, t0 in enumerate(starts):
                total = totals[:, c * POOL_GROUP_DIM:(c + 1) * POOL_GROUP_DIM]
                centre = a_ext[HALO + t0:HALO + t0 + CHUNK, lanes]
                pooled = total * (1.0 / window) - centre
                if r0 + t0 == 0:
                    head = total[:HALO] * inv_count(0, half) - centre[:HALO]
                    pooled = jnp.concatenate([head, pooled[HALO:]], axis=0)
                if r0 + t0 + CHUNK == SEQ_TILE:
                    tail = total[CHUNK - HALO:] * inv_count(SEQ_TILE - HALO, half) - centre[CHUNK - HALO:]
                    pooled = jnp.concatenate([pooled[:CHUNK - HALO], tail], axis=0)
                rows.append(pooled)
            groups.append(jnp.concatenate(rows, axis=0))
        return jnp.concatenate(groups, axis=1)

    def gate_block(vn):
        n_chunks = MIX_SUB // CHUNK
        mixed = []
        for hd in range(SGU_HEADS):
            lanes = slice(hd * SGU_HEAD_DIM, (hd + 1) * SGU_HEAD_DIM)
            blocks = jnp.concatenate([vn[c * CHUNK:(c + 1) * CHUNK, lanes] for c in range(n_chunks)], axis=1)
            mixed.append(_bdot(sguw_ref[hd], blocks))
        return jnp.concatenate(
            [jnp.concatenate([m[:, c * SGU_HEAD_DIM:(c + 1) * SGU_HEAD_DIM] for m in mixed], axis=1) + sbias_ref[...]
             for c in range(n_chunks)], axis=0)

    for r0 in range(0, SEQ_TILE, MIX_SUB):
        pooled = pooled_block(r0, z[r0:r0 + MIX_SUB + 2 * HALO, :POOL_WIDTH])
        uv = z[HALO + r0:HALO + r0 + MIX_SUB, POOL_WIDTH:]
        uv = 0.5 * uv * (1.0 + lax.erf(uv * (0.5 ** 0.5)))
        u, v = uv[:, :SGU_WIDTH], uv[:, SGU_WIDTH:]
        gated = u * gate_block(_layer_norm(v, sg_ref[...], sb_ref[...]).astype(BF16))
        y = _bdot(jnp.concatenate([pooled, gated], axis=1).astype(BF16), outw_ref[...])
        r = ALPHA * x_ref[0, pl.ds(r0, MIX_SUB), :] + gt_ref[0] * y
        o_ref[0, pl.ds(r0, MIX_SUB), :] = _layer_norm(r, g_ref[...], b_ref[...])


def _pool_bands():
    p = lax.broadcasted_iota(jnp.int32, (CHUNK, CHUNK + 2 * HALO), 0) + HALO
    j = lax.broadcasted_iota(jnp.int32, (CHUNK, CHUNK + 2 * HALO), 1)
    return jnp.stack([((j >= p - w // 2) & (j < p + w // 2)).astype(BF16) for w in POOL_WINDOWS])


def _fold_body(pw_ref, ps_ref, ow_ref, o_ref):
    g = pl.program_id(0)

    @pl.when(g < N_POOL_GROUPS)
    def _():
        o_ref[...] = jnp.dot(pw_ref[0] * ps_ref[0], ow_ref[...], preferred_element_type=F32,
                             precision=lax.Precision.HIGHEST).astype(BF16)

    @pl.when(g >= N_POOL_GROUPS)
    def _():
        o_ref[...] = ow_ref[...].astype(BF16)


def _fold_pool_into_out(pool_w, pool_scale, out_w):
    rows, d = out_w.shape
    last = N_POOL_GROUPS - 1
    return pl.pallas_call(
        _fold_body,
        grid=(rows // POOL_GROUP_DIM,),
        in_specs=[pl.BlockSpec((1, POOL_GROUP_DIM, POOL_GROUP_DIM), lambda g: (jnp.minimum(g, last), 0, 0)),
                  pl.BlockSpec((1, 1, POOL_GROUP_DIM), lambda g: (jnp.minimum(g, last), 0, 0)),
                  pl.BlockSpec((POOL_GROUP_DIM, d), lambda g: (g, 0))],
        out_specs=pl.BlockSpec((POOL_GROUP_DIM, d), lambda g: (g, 0)),
        out_shape=jax.ShapeDtypeStruct((rows, d), BF16),
        name="fold_pool",
    )(pool_w, pool_scale.reshape(N_POOL_GROUPS, 1, POOL_GROUP_DIM), out_w)


def _mixer0(x, mods, row_off, in_w, sg, sb, sgu_w, sgu_bias, out_w, g, b):
    bsz, seq, d = x.shape
    prev, main, nxt = _x_specs(seq)
    band = _pool_bands()
    return pl.pallas_call(
        functools.partial(_mix0_body, seq),
        grid=(bsz, seq // SEQ_TILE),
        in_specs=[prev, main, nxt, _mod_spec(row_off), _mod_spec(row_off), _mod_spec(row_off),
                  _const_spec(in_w.shape),
                  _const_spec(band.shape), _const_spec(sg.shape), _const_spec(sb.shape), _const_spec(sgu_w.shape),
                  _const_spec(sgu_bias.shape), _const_spec(out_w.shape),
                  _const_spec(g.shape), _const_spec(b.shape)],
        out_specs=pl.BlockSpec((1, SEQ_TILE, d), lambda bb, i: (bb, i, 0)),
        out_shape=jax.ShapeDtypeStruct(x.shape, F32),
        compiler_params=pltpu.CompilerParams(vmem_limit_bytes=VMEM_LIMIT_BYTES),
        name="mixer0",
    )(x, x, x, mods[0], mods[1], mods[2], in_w, band, sg, sb, sgu_w, sgu_bias, out_w, g, b)


def _mix1_body(seq, xp_ref, x_ref, xn_ref, sh_ref, sc_ref, gt_ref, pw1_ref, pb1_ref, dww_ref, dwb_ref,
               cg_ref, cb_ref, pw2_ref, pb2_ref, g_ref, b_ref, o_ref, g_s, ge_s, go_s, d_s):
    h_ext = _modulated_ext(xp_ref, x_ref, xn_ref, sh_ref, sc_ref)
    ext = SEQ_TILE + 2 * HALO
    valid = _row_valid(seq)

    for blk in range(CONV_WIDTH // GLU_COLS):
        vcols = slice(blk * GLU_COLS, (blk + 1) * GLU_COLS)
        gcols = slice(CONV_WIDTH + blk * GLU_COLS, CONV_WIDTH + (blk + 1) * GLU_COLS)
        value = _bdot(h_ext, pw1_ref[:, vcols]) + pb1_ref[:, vcols]
        gate = _bdot(h_ext, pw1_ref[:, gcols]) + pb1_ref[:, gcols]
        glu = value * _sigmoid(gate)
        glu = jnp.concatenate([jnp.where(valid[:HALO], glu[:HALO], 0.0), glu[HALO:HALO + SEQ_TILE],
                               jnp.where(valid[HALO + SEQ_TILE:], glu[HALO + SEQ_TILE:], 0.0)], axis=0)
        for i in range(ext // SUBLANES):
            for jj in range(GLU_COLS // LANES):
                j = blk * (GLU_COLS // LANES) + jj
                g_s[pl.ds(i * SUBLANES * LANE_BLOCKS + j, SUBLANES, stride=LANE_BLOCKS), :] = (
                    glu[i * SUBLANES:(i + 1) * SUBLANES, jj * LANES:(jj + 1) * LANES])

    pair_rows = 2 * LANE_BLOCKS
    gt = g_s[...]
    ge_s[...] = gt.reshape(ext // 2, pair_rows, LANES).astype(BF16)
    go_s[pl.ds(0, ext // 2 - 1)] = (
        gt[LANE_BLOCKS:ext * LANE_BLOCKS - LANE_BLOCKS].reshape(ext // 2 - 1, pair_rows, LANES).astype(BF16))

    first = HALO - CONV_TAPS // 2
    n_tiles = CONV_PAIRS + (first + CONV_TAPS - 1) // 2

    def conv_step(i, carry):
        q0 = i * CONV_PAIRS
        te = [ge_s[q0 + n] for n in range(n_tiles)]
        to = [go_s[q0 + n] for n in range(n_tiles)]
        accs = [None] * CONV_PAIRS
        for k in range(CONV_TAPS):
            off = first + k
            wk = dww_ref[k].astype(F32)
            for q in range(CONV_PAIRS):
                tile = (te if off % 2 == 0 else to)[q + off // 2]
                prod = wk * tile.astype(F32)
                accs[q] = prod if accs[q] is None else accs[q] + prod
        for q in range(CONV_PAIRS):
            row = pl.multiple_of((q0 + q) * pair_rows, pair_rows)
            d_s[pl.ds(row, pair_rows), :] = accs[q] + dwb_ref[...]
        return carry

    lax.fori_loop(0, SEQ_TILE // (2 * CONV_PAIRS), conv_step, 0)

    for r0 in range(0, SEQ_TILE, MIX_SUB):
        d = jnp.concatenate(
            [jnp.concatenate([d_s[pl.ds(i * SUBLANES * LANE_BLOCKS + j, SUBLANES, stride=LANE_BLOCKS), :]
                              for j in range(LANE_BLOCKS)], axis=1)
             for i in range(r0 // SUBLANES, (r0 + MIX_SUB) // SUBLANES)], axis=0)
        dn = _layer_norm(d, cg_ref[...], cb_ref[...]).astype(BF16)
        dn = dn * _sigmoid(dn)
        y = _bdot(dn, pw2_ref[...]) + pb2_ref[...]
        r = ALPHA * x_ref[0, pl.ds(r0, MIX_SUB), :] + gt_ref[0] * y
        o_ref[0, pl.ds(r0, MIX_SUB), :] = _layer_norm(r, g_ref[...], b_ref[...])


def _mixer1(x, mods, row_off, pw1_w, pw1_b, dw_w, dw_b, cg, cb, pw2_w, pw2_b, g, b):
    bsz, seq, d = x.shape
    prev, main, nxt = _x_specs(seq)
    return pl.pallas_call(
        functools.partial(_mix1_body, seq),
        grid=(bsz, seq // SEQ_TILE),
        in_specs=[prev, main, nxt, _mod_spec(row_off), _mod_spec(row_off), _mod_spec(row_off),
                  _const_spec(pw1_w.shape), _const_spec(pw1_b.shape), _const_spec(dw_w.shape),
                  _const_spec(dw_b.shape), _const_spec(cg.shape), _const_spec(cb.shape),
                  _const_spec(pw2_w.shape), _const_spec(pw2_b.shape),
                  _const_spec(g.shape), _const_spec(b.shape)],
        out_specs=pl.BlockSpec((1, SEQ_TILE, d), lambda bb, i: (bb, i, 0)),
        out_shape=jax.ShapeDtypeStruct(x.shape, F32),
        scratch_shapes=[pltpu.VMEM(((SEQ_TILE + 2 * HALO) * LANE_BLOCKS, LANES), F32),
                        pltpu.VMEM(((SEQ_TILE + 2 * HALO) // 2, 2 * LANE_BLOCKS, LANES), BF16),
                        pltpu.VMEM(((SEQ_TILE + 2 * HALO) // 2, 2 * LANE_BLOCKS, LANES), BF16),
                        pltpu.VMEM((SEQ_TILE * LANE_BLOCKS, LANES), F32)],
        compiler_params=pltpu.CompilerParams(vmem_limit_bytes=VMEM_LIMIT_BYTES),
        name="mixer1",
    )(x, x, x, mods[0], mods[1], mods[2], pw1_w, pw1_b, dw_w, dw_b, cg, cb, pw2_w, pw2_b, g, b)


def _mlp_body(x_ref, sh_ref, sc_ref, gt_ref, w1_ref, w2_ref, g_ref, b_ref, o_ref):
    for s in range(MLP_TILE // MLP_SUB):
        rows = pl.ds(s * MLP_SUB, MLP_SUB)
        x = x_ref[0, rows, :]
        h = (x * (1.0 + sc_ref[0]) + sh_ref[0]).astype(BF16)
        acc = None
        for f in range(D_FF // FF_CHUNK):
            cols = slice(f * FF_CHUNK, (f + 1) * FF_CHUNK)
            u = jnp.maximum(_bdot(h, w1_ref[:, cols]), 0.0)
            part = _bdot((u * u).astype(BF16), w2_ref[cols, :])
            acc = part if acc is None else acc + part
        r = ALPHA * x + gt_ref[0] * acc
        o_ref[0, rows, :] = _layer_norm(r, g_ref[...], b_ref[...])


def _mlp(x, mods, row_off, w1, w2, g, b):
    bsz, seq, d = x.shape
    tile = pl.BlockSpec((1, MLP_TILE, d), lambda bb, i: (bb, i, 0))
    return pl.pallas_call(
        _mlp_body,
        grid=(bsz, seq // MLP_TILE),
        in_specs=[tile, _mod_spec(row_off), _mod_spec(row_off), _mod_spec(row_off),
                  _const_spec(w1.shape), _const_spec(w2.shape), _const_spec(g.shape), _const_spec(b.shape)],
        out_specs=tile,
        out_shape=jax.ShapeDtypeStruct(x.shape, F32),
        compiler_params=pltpu.CompilerParams(vmem_limit_bytes=VMEM_LIMIT_BYTES),
        name="mlp",
    )(x, mods[3], mods[4], mods[5], w1, w2, g, b)


def _row(v):
    return v.reshape(1, -1)


def kernel(x_prompt, x_sample, c_prompt, c_sample, l0_ada_w, l0_ada_b, l0_in_w, l0_pool_w, l0_pool_scale, l0_sgu_ln_g, l0_sgu_ln_b, l0_sgu_w, l0_sgu_b, l0_out_w, l0_ln1_g, l0_ln1_b, l0_mlp_w1, l0_mlp_w2, l0_ln2_g, l0_ln2_b, l1_ada_w, l1_ada_b, l1_pw1_w, l1_pw1_b, l1_dw_w, l1_dw_b, l1_cnorm_g, l1_cnorm_b, l1_pw2_w, l1_pw2_b, l1_ln1_g, l1_ln1_b, l1_mlp_w1, l1_mlp_w2, l1_ln2_g, l1_ln2_b):
    n_prompt = c_prompt.shape[0]
    c_all = jnp.concatenate([c_prompt, c_sample], axis=0)

    def mods_of(ada_w, ada_b):
        mod = _adaln(c_all, ada_w, ada_b)
        return [m.reshape(-1, 1, D_MODEL) for m in jnp.split(mod, N_MOD, axis=-1)]

    mods0 = mods_of(l0_ada_w, l0_ada_b)
    mods1 = mods_of(l1_ada_w, l1_ada_b)

    sgu_bias = jnp.repeat(l0_sgu_b.T, SGU_HEAD_DIM, axis=1)

    in_w = l0_in_w.astype(BF16)
    sgu_w = l0_sgu_w.astype(BF16)
    out_w = _fold_pool_into_out(l0_pool_w, l0_pool_scale, l0_out_w)
    w1_0, w2_0 = l0_mlp_w1.astype(BF16), l0_mlp_w2.astype(BF16)
    pw1_w, pw2_w = l1_pw1_w.astype(BF16), l1_pw2_w.astype(BF16)
    dw_pair = jnp.tile(l1_dw_w.reshape(CONV_TAPS, 1, LANE_BLOCKS, LANES), (1, 2, 1, 1))
    dw_pair = dw_pair.reshape(CONV_TAPS, 2 * LANE_BLOCKS, LANES).astype(BF16)
    dwb_pair = jnp.tile(l1_dw_b.reshape(LANE_BLOCKS, LANES), (2, 1))
    w1_1, w2_1 = l1_mlp_w1.astype(BF16), l1_mlp_w2.astype(BF16)

    def trunk(x, row_off):
        x = _mixer0(x, mods0, row_off, in_w, _row(l0_sgu_ln_g), _row(l0_sgu_ln_b),
                    sgu_w, sgu_bias, out_w, _row(l0_ln1_g), _row(l0_ln1_b))
        x = _mlp(x, mods0, row_off, w1_0, w2_0, _row(l0_ln2_g), _row(l0_ln2_b))
        x = _mixer1(x, mods1, row_off, pw1_w, _row(l1_pw1_b), dw_pair, dwb_pair, _row(l1_cnorm_g),
                    _row(l1_cnorm_b), pw2_w, _row(l1_pw2_b), _row(l1_ln1_g), _row(l1_ln1_b))
        x = _mlp(x, mods1, row_off, w1_1, w2_1, _row(l1_ln2_g), _row(l1_ln2_b))
        return x

    return (trunk(x_prompt, 0), trunk(x_sample, n_prompt))
```

```python
import functools

import jax
import jax.numpy as jnp
from jax import lax
from jax.experimental import pallas as pl
from jax.experimental.pallas import tpu as pltpu

D_MODEL = 1024
DEPTH = 2
POOL_WINDOWS = (2, 4, 8, 16)
N_POOL_GROUPS = 4
POOL_GROUP_DIM = 128
POOL_WIDTH = 512
SGU_HEADS = 4
SGU_HEAD_DIM = 128
SGU_WIDTH = 512
CHUNK = 128
IN_WIDTH = POOL_WIDTH + 2 * SGU_WIDTH
CONV_WIDTH = D_MODEL
CONV_TAPS = 31
D_FF = 4 * D_MODEL
N_MOD = 6
ALPHA = (2.0 * DEPTH) ** 0.25
LN_EPS = 1e-5

F32 = jnp.float32
BF16 = jnp.bfloat16

V7X_VMEM_BYTES = 64 * 1024 * 1024
VMEM_LIMIT_BYTES = V7X_VMEM_BYTES - 8 * 1024 * 1024

SEQ_TILE = 1024
HALO = 16
FF_CHUNK = 1024
MLP_TILE = 1024
MLP_SUB = 256
MIX_SUB = 256
SUBLANES = 8
LANES = 128
LANE_BLOCKS = CONV_WIDTH // LANES
assert LANE_BLOCKS == SUBLANES
CONV_PAIRS = 32
GLU_COLS = 256
ADA_COLS = 1536


def _layer_norm(r, g, b):
    mu = jnp.mean(r, axis=-1, keepdims=True)
    d = r - mu
    var = jnp.mean(d * d, axis=-1, keepdims=True)
    return d * lax.rsqrt(var + LN_EPS) * g + b


def _sigmoid(x):
    return 1.0 / (1.0 + jnp.exp(-x))


def _bdot(a, b):
    return jnp.dot(a, b, preferred_element_type=F32)


def _ada_body(c_ref, w_ref, b_ref, o_ref):
    c = c_ref[...]
    s = (c * _sigmoid(c)).astype(BF16)
    o_ref[...] = _bdot(s, w_ref[...].astype(BF16)) + b_ref[...]


def _adaln(c, w, b):
    m, d = c.shape
    n = w.shape[1]
    return pl.pallas_call(
        _ada_body,
        grid=(n // ADA_COLS,),
        in_specs=[
            pl.BlockSpec((m, d), lambda j: (0, 0)),
            pl.BlockSpec((d, ADA_COLS), lambda j: (0, j)),
            pl.BlockSpec((1, ADA_COLS), lambda j: (0, j)),
        ],
        out_specs=pl.BlockSpec((m, ADA_COLS), lambda j: (0, j)),
        out_shape=jax.ShapeDtypeStruct((m, n), F32),
        compiler_params=pltpu.CompilerParams(vmem_limit_bytes=VMEM_LIMIT_BYTES),
        name="adaln",
    )(c, w, b.reshape(1, n))


def _const_spec(shape):
    nd = len(shape)
    return pl.BlockSpec(shape, lambda b, i: (0,) * nd, pipeline_mode=pl.Buffered(1))


def _mod_spec(row_off):
    return pl.BlockSpec((1, 1, D_MODEL), lambda b, i: (b + row_off, 0, 0))


def _x_specs(seq):
    per_tile = SEQ_TILE // HALO
    last = seq // HALO - 1
    prev = pl.BlockSpec((1, HALO, D_MODEL), lambda b, i: (b, jnp.maximum(i * per_tile - 1, 0), 0))
    main = pl.BlockSpec((1, SEQ_TILE, D_MODEL), lambda b, i: (b, i, 0))
    nxt = pl.BlockSpec((1, HALO, D_MODEL), lambda b, i: (b, jnp.minimum((i + 1) * per_tile, last), 0))
    return prev, main, nxt


def _modulated_ext(xp_ref, x_ref, xn_ref, sh_ref, sc_ref):
    x_ext = jnp.concatenate([xp_ref[0], x_ref[0], xn_ref[0]], axis=0)
    return (x_ext * (1.0 + sc_ref[0]) + sh_ref[0]).astype(BF16)


def _row_valid(seq):
    i = pl.program_id(1)
    pos = i * SEQ_TILE - HALO + lax.broadcasted_iota(jnp.int32, (SEQ_TILE + 2 * HALO, 1), 0)
    return (pos >= 0) & (pos < seq)


def _mix0_body(seq, xp_ref, x_ref, xn_ref, sh_ref, sc_ref, gt_ref, inw_ref,
               band_ref, sg_ref, sb_ref, sguw_ref, sbias_ref, outw_ref, g_ref, b_ref, o_ref):
    tile_pos = pl.program_id(1) * SEQ_TILE
    h_ext = _modulated_ext(xp_ref, x_ref, xn_ref, sh_ref, sc_ref)
    z = _bdot(h_ext, inw_ref[...])
    valid = _row_valid(seq)

    def inv_count(row0, half):
        pos = tile_pos + row0 + lax.broadcasted_iota(jnp.int32, (HALO, 1), 0)
        cnt = jnp.minimum(pos + half, seq) - jnp.maximum(pos - half, 0)
        return 1.0 / cnt.astype(F32)

    def pooled_block(r0, a_ext):
        if r0 == 0:
            a_ext = jnp.concatenate([jnp.where(valid[:HALO], a_ext[:HALO], 0.0), a_ext[HALO:]], axis=0)
        if r0 + MIX_SUB == SEQ_TILE:
            a_ext = jnp.concatenate([a_ext[:HALO + MIX_SUB],
                                     jnp.where(valid[HALO + SEQ_TILE:], a_ext[HALO + MIX_SUB:], 0.0)], axis=0)
        a_bf = a_ext.astype(BF16)
        starts = range(0, MIX_SUB, CHUNK)
        groups = []
        for grp, window in enumerate(POOL_WINDOWS):
            half = window // 2
            lanes = slice(grp * POOL_GROUP_DIM, (grp + 1) * POOL_GROUP_DIM)
            windows = jnp.concatenate([a_bf[t0:t0 + CHUNK + 2 * HALO, lanes] for t0 in starts], axis=1)
            totals = _bdot(band_ref[grp], windows)
            rows = []
            for c, t0 in enumerate(starts):
                total = totals[:, c * POOL_GROUP_DIM:(c + 1) * POOL_GROUP_DIM]
                centre = a_ext[HALO + t0:HALO + t0 + CHUNK, lanes]
                pooled = total * (1.0 / window) - centre
                if r0 + t0 == 0:
                    head = total[:HALO] * inv_count(0, half) - centre[:HALO]
                    pooled = jnp.concatenate([head, pooled[HALO:]], axis=0)
                if r0 + t0 + CHUNK == SEQ_TILE:
                    tail = total[CHUNK - HALO:] * inv_count(SEQ_TILE - HALO, half) - centre[CHUNK - HALO:]
                    pooled = jnp.concatenate([pooled[:CHUNK - HALO], tail], axis=0)
                rows.append(pooled)
            groups.append(jnp.concatenate(rows, axis=0))
        return jnp.concatenate(groups, axis=1)

    def gate_block(vn):
        n_chunks = MIX_SUB // CHUNK
        mixed = []
        for hd in range(SGU_HEADS):
            lanes = slice(hd * SGU_HEAD_DIM, (hd + 1) * SGU_HEAD_DIM)
            blocks = jnp.concatenate([vn[c * CHUNK:(c + 1) * CHUNK, lanes] for c in range(n_chunks)], axis=1)
            mixed.append(_bdot(sguw_ref[hd], blocks))
        return jnp.concatenate(
            [jnp.concatenate([m[:, c * SGU_HEAD_DIM:(c + 1) * SGU_HEAD_DIM] for m in mixed], axis=1) + sbias_ref[...]
             for c in range(n_chunks)], axis=0)

    for r0 in range(0, SEQ_TILE, MIX_SUB):
        pooled = pooled_block(r0, z[r0:r0 + MIX_SUB + 2 * HALO, :POOL_WIDTH])
        uv = z[HALO + r0:HALO + r0 + MIX_SUB, POOL_WIDTH:]
        uv = 0.5 * uv * (1.0 + lax.erf(uv * (0.5 ** 0.5)))
        u, v = uv[:, :SGU_WIDTH], uv[:, SGU_WIDTH:]
        gated = u * gate_block(_layer_norm(v, sg_ref[...], sb_ref[...]).astype(BF16))
        y = _bdot(jnp.concatenate([pooled, gated], axis=1).astype(BF16), outw_ref[...])
        r = ALPHA * x_ref[0, pl.ds(r0, MIX_SUB), :] + gt_ref[0] * y
        o_ref[0, pl.ds(r0, MIX_SUB), :] = _layer_norm(r, g_ref[...], b_ref[...])


def _pool_bands():
    p = lax.broadcasted_iota(jnp.int32, (CHUNK, CHUNK + 2 * HALO), 0) + HALO
    j = lax.broadcasted_iota(jnp.int32, (CHUNK, CHUNK + 2 * HALO), 1)
    return jnp.stack([((j >= p - w // 2) & (j < p + w // 2)).astype(BF16) for w in POOL_WINDOWS])


def _fold_body(pw_ref, ps_ref, ow_ref, o_ref):
    g = pl.program_id(0)

    @pl.when(g < N_POOL_GROUPS)
    def _():
        o_ref[...] = jnp.dot(pw_ref[0] * ps_ref[0], ow_ref[...], preferred_element_type=F32,
                             precision=lax.Precision.HIGHEST).astype(BF16)

    @pl.when(g >= N_POOL_GROUPS)
    def _():
        o_ref[...] = ow_ref[...].astype(BF16)


def _fold_pool_into_out(pool_w, pool_scale, out_w):
    rows, d = out_w.shape
    last = N_POOL_GROUPS - 1
    return pl.pallas_call(
        _fold_body,
        grid=(rows // POOL_GROUP_DIM,),
        in_specs=[pl.BlockSpec((1, POOL_GROUP_DIM, POOL_GROUP_DIM), lambda g: (jnp.minimum(g, last), 0, 0)),
                  pl.BlockSpec((1, 1, POOL_GROUP_DIM), lambda g: (jnp.minimum(g, last), 0, 0)),
                  pl.BlockSpec((POOL_GROUP_DIM, d), lambda g: (g, 0))],
        out_specs=pl.BlockSpec((POOL_GROUP_DIM, d), lambda g: (g, 0)),
        out_shape=jax.ShapeDtypeStruct((rows, d), BF16),
        name="fold_pool",
    )(pool_w, pool_scale.reshape(N_POOL_GROUPS, 1, POOL_GROUP_DIM), out_w)


def _mixer0(x, mods, row_off, in_w, sg, sb, sgu_w, sgu_bias, out_w, g, b):
    bsz, seq, d = x.shape
    prev, main, nxt = _x_specs(seq)
    band = _pool_bands()
    return pl.pallas_call(
        functools.partial(_mix0_body, seq),
        grid=(bsz, seq // SEQ_TILE),
        in_specs=[prev, main, nxt, _mod_spec(row_off), _mod_spec(row_off), _mod_spec(row_off),
                  _const_spec(in_w.shape),
                  _const_spec(band.shape), _const_spec(sg.shape), _const_spec(sb.shape), _const_spec(sgu_w.shape),
                  _const_spec(sgu_bias.shape), _const_spec(out_w.shape),
                  _const_spec(g.shape), _const_spec(b.shape)],
        out_specs=pl.BlockSpec((1, SEQ_TILE, d), lambda bb, i: (bb, i, 0)),
        out_shape=jax.ShapeDtypeStruct(x.shape, F32),
        compiler_params=pltpu.CompilerParams(vmem_limit_bytes=VMEM_LIMIT_BYTES),
        name="mixer0",
    )(x, x, x, mods[0], mods[1], mods[2], in_w, band, sg, sb, sgu_w, sgu_bias, out_w, g, b)


def _mix1_body(seq, xp_ref, x_ref, xn_ref, sh_ref, sc_ref, gt_ref, pw1_ref, pb1_ref, dww_ref, dwb_ref,
               cg_ref, cb_ref, pw2_ref, pb2_ref, g_ref, b_ref, o_ref, g_s, ge_s, go_s, d_s):
    h_ext = _modulated_ext(xp_ref, x_ref, xn_ref, sh_ref, sc_ref)
    ext = SEQ_TILE + 2 * HALO
    valid = _row_valid(seq)

    for blk in range(CONV_WIDTH // GLU_COLS):
        vcols = slice(blk * GLU_COLS, (blk + 1) * GLU_COLS)
        gcols = slice(CONV_WIDTH + blk * GLU_COLS, CONV_WIDTH + (blk + 1) * GLU_COLS)
        value = _bdot(h_ext, pw1_ref[:, vcols]) + pb1_ref[:, vcols]
        gate = _bdot(h_ext, pw1_ref[:, gcols]) + pb1_ref[:, gcols]
        glu = value * _sigmoid(gate)
        glu = jnp.concatenate([jnp.where(valid[:HALO], glu[:HALO], 0.0), glu[HALO:HALO + SEQ_TILE],
                               jnp.where(valid[HALO + SEQ_TILE:], glu[HALO + SEQ_TILE:], 0.0)], axis=0)
        for i in range(ext // SUBLANES):
            for jj in range(GLU_COLS // LANES):
                j = blk * (GLU_COLS // LANES) + jj
                g_s[pl.ds(i * SUBLANES * LANE_BLOCKS + j, SUBLANES, stride=LANE_BLOCKS), :] = (
                    glu[i * SUBLANES:(i + 1) * SUBLANES, jj * LANES:(jj + 1) * LANES])

    pair_rows = 2 * LANE_BLOCKS
    gt = g_s[...]
    ge_s[...] = gt.reshape(ext // 2, pair_rows, LANES).astype(BF16)
    go_s[pl.ds(0, ext // 2 - 1)] = (
        gt[LANE_BLOCKS:ext * LANE_BLOCKS - LANE_BLOCKS].reshape(ext // 2 - 1, pair_rows, LANES).astype(BF16))

    first = HALO - CONV_TAPS // 2
    n_tiles = CONV_PAIRS + (first + CONV_TAPS - 1) // 2

    def conv_step(i, carry):
        q0 = i * CONV_PAIRS
        te = [ge_s[q0 + n] for n in range(n_tiles)]
        to = [go_s[q0 + n] for n in range(n_tiles)]
        accs = [None] * CONV_PAIRS
        for k in range(CONV_TAPS):
            off = first + k
            wk = dww_ref[k].astype(F32)
            for q in range(CONV_PAIRS):
                tile = (te if off % 2 == 0 else to)[q + off // 2]
                prod = wk * tile.astype(F32)
                accs[q] = prod if accs[q] is None else accs[q] + prod
        for q in range(CONV_PAIRS):
            row = pl.multiple_of((q0 + q) * pair_rows, pair_rows)
            d_s[pl.ds(row, pair_rows), :] = accs[q] + dwb_ref[...]
        return carry

    lax.fori_loop(0, SEQ_TILE // (2 * CONV_PAIRS), conv_step, 0)

    for r0 in range(0, SEQ_TILE, MIX_SUB):
        d = jnp.concatenate(
            [jnp.concatenate([d_s[pl.ds(i * SUBLANES * LANE_BLOCKS + j, SUBLANES, stride=LANE_BLOCKS), :]
                              for j in range(LANE_BLOCKS)], axis=1)
             for i in range(r0 // SUBLANES, (r0 + MIX_SUB) // SUBLANES)], axis=0)
        dn = _layer_norm(d, cg_ref[...], cb_ref[...]).astype(BF16)
        dn = dn * _sigmoid(dn)
        y = _bdot(dn, pw2_ref[...]) + pb2_ref[...]
        r = ALPHA * x_ref[0, pl.ds(r0, MIX_SUB), :] + gt_ref[0] * y
        o_ref[0, pl.ds(r0, MIX_SUB), :] = _layer_norm(r, g_ref[...], b_ref[...])


def _mixer1(x, mods, row_off, pw1_w, pw1_b, dw_w, dw_b, cg, cb, pw2_w, pw2_b, g, b):
    bsz, seq, d = x.shape
    prev, main, nxt = _x_specs(seq)
    return pl.pallas_call(
        functools.partial(_mix1_body, seq),
        grid=(bsz, seq // SEQ_TILE),
        in_specs=[prev, main, nxt, _mod_spec(row_off), _mod_spec(row_off), _mod_spec(row_off),
                  _const_spec(pw1_w.shape), _const_spec(pw1_b.shape), _const_spec(dw_w.shape),
                  _const_spec(dw_b.shape), _const_spec(cg.shape), _const_spec(cb.shape),
                  _const_spec(pw2_w.shape), _const_spec(pw2_b.shape),
                  _const_spec(g.shape), _const_spec(b.shape)],
        out_specs=pl.BlockSpec((1, SEQ_TILE, d), lambda bb, i: (bb, i, 0)),
        out_shape=jax.ShapeDtypeStruct(x.shape, F32),
        scratch_shapes=[pltpu.VMEM(((SEQ_TILE + 2 * HALO) * LANE_BLOCKS, LANES), F32),
                        pltpu.VMEM(((SEQ_TILE + 2 * HALO) // 2, 2 * LANE_BLOCKS, LANES), BF16),
                        pltpu.VMEM(((SEQ_TILE + 2 * HALO) // 2, 2 * LANE_BLOCKS, LANES), BF16),
                        pltpu.VMEM((SEQ_TILE * LANE_BLOCKS, LANES), F32)],
        compiler_params=pltpu.CompilerParams(vmem_limit_bytes=VMEM_LIMIT_BYTES),
        name="mixer1",
    )(x, x, x, mods[0], mods[1], mods[2], pw1_w, pw1_b, dw_w, dw_b, cg, cb, pw2_w, pw2_b, g, b)


def _mlp_body(x_ref, sh_ref, sc_ref, gt_ref, w1_ref, w2_ref, g_ref, b_ref, o_ref):
    for s in range(MLP_TILE // MLP_SUB):
        rows = pl.ds(s * MLP_SUB, MLP_SUB)
        x = x_ref[0, rows, :]
        h = (x * (1.0 + sc_ref[0]) + sh_ref[0]).astype(BF16)
        acc = None
        for f in range(D_FF // FF_CHUNK):
            cols = slice(f * FF_CHUNK, (f + 1) * FF_CHUNK)
            u = jnp.maximum(_bdot(h, w1_ref[:, cols]), 0.0)
            part = _bdot((u * u).astype(BF16), w2_ref[cols, :])
            acc = part if acc is None else acc + part
        r = ALPHA * x + gt_ref[0] * acc
        o_ref[0, rows, :] = _layer_norm(r, g_ref[...], b_ref[...])


def _mlp(x, mods, row_off, w1, w2, g, b):
    bsz, seq, d = x.shape
    tile = pl.BlockSpec((1, MLP_TILE, d), lambda bb, i: (bb, i, 0))
    return pl.pallas_call(
        _mlp_body,
        grid=(bsz, seq // MLP_TILE),
        in_specs=[tile, _mod_spec(row_off), _mod_spec(row_off), _mod_spec(row_off),
                  _const_spec(w1.shape), _const_spec(w2.shape), _const_spec(g.shape), _const_spec(b.shape)],
        out_specs=tile,
        out_shape=jax.ShapeDtypeStruct(x.shape, F32),
        compiler_params=pltpu.CompilerParams(vmem_limit_bytes=VMEM_LIMIT_BYTES),
        name="mlp",
    )(x, mods[3], mods[4], mods[5], w1, w2, g, b)


def _row(v):
    return v.reshape(1, -1)


def kernel(x_prompt, x_sample, c_prompt, c_sample, l0_ada_w, l0_ada_b, l0_in_w, l0_pool_w, l0_pool_scale, l0_sgu_ln_g, l0_sgu_ln_b, l0_sgu_w, l0_sgu_b, l0_out_w, l0_ln1_g, l0_ln1_b, l0_mlp_w1, l0_mlp_w2, l0_ln2_g, l0_ln2_b, l1_ada_w, l1_ada_b, l1_pw1_w, l1_pw1_b, l1_dw_w, l1_dw_b, l1_cnorm_g, l1_cnorm_b, l1_pw2_w, l1_pw2_b, l1_ln1_g, l1_ln1_b, l1_mlp_w1, l1_mlp_w2, l1_ln2_g, l1_ln2_b):
    n_prompt = c_prompt.shape[0]
    c_all = jnp.concatenate([c_prompt, c_sample], axis=0)

    def mods_of(ada_w, ada_b):
        mod = _adaln(c_all, ada_w, ada_b)
        return [m.reshape(-1, 1, D_MODEL) for m in jnp.split(mod, N_MOD, axis=-1)]

    mods0 = mods_of(l0_ada_w, l0_ada_b)
    mods1 = mods_of(l1_ada_w, l1_ada_b)

    sgu_bias = jnp.repeat(l0_sgu_b.T, SGU_HEAD_DIM, axis=1)

    in_w = l0_in_w.astype(BF16)
    sgu_w = l0_sgu_w.astype(BF16)
    out_w = _fold_pool_into_out(l0_pool_w, l0_pool_scale, l0_out_w)
    w1_0, w2_0 = l0_mlp_w1.astype(BF16), l0_mlp_w2.astype(BF16)
    pw1_w, pw2_w = l1_pw1_w.astype(BF16), l1_pw2_w.astype(BF16)
    dw_pair = jnp.tile(l1_dw_w.reshape(CONV_TAPS, 1, LANE_BLOCKS, LANES), (1, 2, 1, 1))
    dw_pair = dw_pair.reshape(CONV_TAPS, 2 * LANE_BLOCKS, LANES).astype(BF16)
    dwb_pair = jnp.tile(l1_dw_b.reshape(LANE_BLOCKS, LANES), (2, 1))
    w1_1, w2_1 = l1_mlp_w1.astype(BF16), l1_mlp_w2.astype(BF16)

    def trunk(x, row_off):
        x = _mixer0(x, mods0, row_off, in_w, _row(l0_sgu_ln_g), _row(l0_sgu_ln_b),
                    sgu_w, sgu_bias, out_w, _row(l0_ln1_g), _row(l0_ln1_b))
        x = _mlp(x, mods0, row_off, w1_0, w2_0, _row(l0_ln2_g), _row(l0_ln2_b))
        x = _mixer1(x, mods1, row_off, pw1_w, _row(l1_pw1_b), dw_pair, dwb_pair, _row(l1_cnorm_g),
                    _row(l1_cnorm_b), pw2_w, _row(l1_pw2_b), _row(l1_ln1_g), _row(l1_ln1_b))
        x = _mlp(x, mods1, row_off, w1_1, w2_1, _row(l1_ln2_g), _row(l1_ln2_b))
        return x

    return (trunk(x_prompt, 0), trunk(x_sample, n_prompt))
```

```python
import functools

import jax
import jax.numpy as jnp
from jax import lax
from jax.experimental import pallas as pl
from jax.experimental.pallas import tpu as pltpu

D_MODEL = 1024
DEPTH = 2
POOL_WINDOWS = (2, 4, 8, 16)
N_POOL_GROUPS = 4
POOL_GROUP_DIM = 128
POOL_WIDTH = 512
SGU_HEADS = 4
SGU_HEAD_DIM = 128
SGU_WIDTH = 512
CHUNK = 128
IN_WIDTH = POOL_WIDTH + 2 * SGU_WIDTH
CONV_WIDTH = D_MODEL
CONV_TAPS = 31
D_FF = 4 * D_MODEL
N_MOD = 6
ALPHA = (2.0 * DEPTH) ** 0.25
LN_EPS = 1e-5

F32 = jnp.float32
BF16 = jnp.bfloat16

V7X_VMEM_BYTES = 64 * 1024 * 1024
VMEM_LIMIT_BYTES = V7X_VMEM_BYTES - 8 * 1024 * 1024

SEQ_TILE = 1024
HALO = 16
FF_CHUNK = 1024
MLP_TILE = 2048
MLP_SUB = 256
MIX_SUB = 256
SUBLANES = 8
LANES = 128
LANE_BLOCKS = CONV_WIDTH // LANES
assert LANE_BLOCKS == SUBLANES
CONV_PAIRS = 32
GLU_COLS = 256
ADA_COLS = 1536


def _layer_norm(r, g, b):
    mu = jnp.mean(r, axis=-1, keepdims=True)
    d = r - mu
    var = jnp.mean(d * d, axis=-1, keepdims=True)
    return d * lax.rsqrt(var + LN_EPS) * g + b


def _sigmoid(x):
    return 1.0 / (1.0 + jnp.exp(-x))


def _bdot(a, b):
    return jnp.dot(a, b, preferred_element_type=F32)


def _ada_body(c_ref, w_ref, b_ref, o_ref):
    c = c_ref[...]
    s = (c * _sigmoid(c)).astype(BF16)
    o_ref[...] = _bdot(s, w_ref[...].astype(BF16)) + b_ref[...]


def _adaln(c, w, b):
    m, d = c.shape
    n = w.shape[1]
    return pl.pallas_call(
        _ada_body,
        grid=(n // ADA_COLS,),
        in_specs=[
            pl.BlockSpec((m, d), lambda j: (0, 0)),
            pl.BlockSpec((d, ADA_COLS), lambda j: (0, j)),
            pl.BlockSpec((1, ADA_COLS), lambda j: (0, j)),
        ],
        out_specs=pl.BlockSpec((m, ADA_COLS), lambda j: (0, j)),
        out_shape=jax.ShapeDtypeStruct((m, n), F32),
        compiler_params=pltpu.CompilerParams(vmem_limit_bytes=VMEM_LIMIT_BYTES),
        name="adaln",
    )(c, w, b.reshape(1, n))


def _const_spec(shape):
    nd = len(shape)
    return pl.BlockSpec(shape, lambda b, i: (0,) * nd, pipeline_mode=pl.Buffered(1))


def _mod_spec(row_off):
    return pl.BlockSpec((1, 1, D_MODEL), lambda b, i: (b + row_off, 0, 0))


def _x_specs(seq):
    per_tile = SEQ_TILE // HALO
    last = seq // HALO - 1
    prev = pl.BlockSpec((1, HALO, D_MODEL), lambda b, i: (b, jnp.maximum(i * per_tile - 1, 0), 0))
    main = pl.BlockSpec((1, SEQ_TILE, D_MODEL), lambda b, i: (b, i, 0))
    nxt = pl.BlockSpec((1, HALO, D_MODEL), lambda b, i: (b, jnp.minimum((i + 1) * per_tile, last), 0))
    return prev, main, nxt


def _modulated_ext(xp_ref, x_ref, xn_ref, sh_ref, sc_ref):
    x_ext = jnp.concatenate([xp_ref[0], x_ref[0], xn_ref[0]], axis=0)
    return (x_ext * (1.0 + sc_ref[0]) + sh_ref[0]).astype(BF16)


def _row_valid(seq):
    i = pl.program_id(1)
    pos = i * SEQ_TILE - HALO + lax.broadcasted_iota(jnp.int32, (SEQ_TILE + 2 * HALO, 1), 0)
    return (pos >= 0) & (pos < seq)


def _mix0_body(seq, xp_ref, x_ref, xn_ref, sh_ref, sc_ref, gt_ref, inw_ref,
               band_ref, sg_ref, sb_ref, sguw_ref, sbias_ref, outw_ref, g_ref, b_ref, o_ref):
    tile_pos = pl.program_id(1) * SEQ_TILE
    h_ext = _modulated_ext(xp_ref, x_ref, xn_ref, sh_ref, sc_ref)
    z = _bdot(h_ext, inw_ref[...])
    valid = _row_valid(seq)

    def inv_count(row0, half):
        pos = tile_pos + row0 + lax.broadcasted_iota(jnp.int32, (HALO, 1), 0)
        cnt = jnp.minimum(pos + half, seq) - jnp.maximum(pos - half, 0)
        return 1.0 / cnt.astype(F32)

    def pooled_block(r0, a_ext):
        if r0 == 0:
            a_ext = jnp.concatenate([jnp.where(valid[:HALO], a_ext[:HALO], 0.0), a_ext[HALO:]], axis=0)
        if r0 + MIX_SUB == SEQ_TILE:
            a_ext = jnp.concatenate([a_ext[:HALO + MIX_SUB],
                                     jnp.where(valid[HALO + SEQ_TILE:], a_ext[HALO + MIX_SUB:], 0.0)], axis=0)
        a_bf = a_ext.astype(BF16)
        starts = range(0, MIX_SUB, CHUNK)
        groups = []
        for grp, window in enumerate(POOL_WINDOWS):
            half = window // 2
            lanes = slice(grp * POOL_GROUP_DIM, (grp + 1) * POOL_GROUP_DIM)
            windows = jnp.concatenate([a_bf[t0:t0 + CHUNK + 2 * HALO, lanes] for t0 in starts], axis=1)
            totals = _bdot(band_ref[grp], windows)
            rows = []
            for c, t0 in enumerate(starts):
                total = totals[:, c * POOL_GROUP_DIM:(c + 1) * POOL_GROUP_DIM]
                centre = a_ext[HALO + t0:HALO + t0 + CHUNK, lanes]
                pooled = total * (1.0 / window) - centre
                if r0 + t0 == 0:
                    head = total[:HALO] * inv_count(0, half) - centre[:HALO]
                    pooled = jnp.concatenate([head, pooled[HALO:]], axis=0)
                if r0 + t0 + CHUNK == SEQ_TILE:
                    tail = total[CHUNK - HALO:] * inv_count(SEQ_TILE - HALO, half) - centre[CHUNK - HALO:]
                    pooled = jnp.concatenate([pooled[:CHUNK - HALO], tail], axis=0)
                rows.append(pooled)
            groups.append(jnp.concatenate(rows, axis=0))
        return jnp.concatenate(groups, axis=1)

    def gate_block(vn):
        n_chunks = MIX_SUB // CHUNK
        mixed = []
        for hd in range(SGU_HEADS):
            lanes = slice(hd * SGU_HEAD_DIM, (hd + 1) * SGU_HEAD_DIM)
            blocks = jnp.concatenate([vn[c * CHUNK:(c + 1) * CHUNK, lanes] for c in range(n_chunks)], axis=1)
            mixed.append(_bdot(sguw_ref[hd], blocks))
        return jnp.concatenate(
            [jnp.concatenate([m[:, c * SGU_HEAD_DIM:(c + 1) * SGU_HEAD_DIM] for m in mixed], axis=1) + sbias_ref[...]
             for c in range(n_chunks)], axis=0)

    for r0 in range(0, SEQ_TILE, MIX_SUB):
        pooled = pooled_block(r0, z[r0:r0 + MIX_SUB + 2 * HALO, :POOL_WIDTH])
        uv = z[HALO + r0:HALO + r0 + MIX_SUB, POOL_WIDTH:]
        uv = 0.5 * uv * (1.0 + lax.erf(uv * (0.5 ** 0.5)))
        u, v = uv[:, :SGU_WIDTH], uv[:, SGU_WIDTH:]
        gated = u * gate_block(_layer_norm(v, sg_ref[...], sb_ref[...]).astype(BF16))
        y = _bdot(jnp.concatenate([pooled, gated], axis=1).astype(BF16), outw_ref[...])
        r = ALPHA * x_ref[0, pl.ds(r0, MIX_SUB), :] + gt_ref[0] * y
        o_ref[0, pl.ds(r0, MIX_SUB), :] = _layer_norm(r, g_ref[...], b_ref[...])


def _pool_bands():
    p = lax.broadcasted_iota(jnp.int32, (CHUNK, CHUNK + 2 * HALO), 0) + HALO
    j = lax.broadcasted_iota(jnp.int32, (CHUNK, CHUNK + 2 * HALO), 1)
    return jnp.stack([((j >= p - w // 2) & (j < p + w // 2)).astype(BF16) for w in POOL_WINDOWS])


def _fold_body(pw_ref, ps_ref, ow_ref, o_ref):
    g = pl.program_id(0)

    @pl.when(g < N_POOL_GROUPS)
    def _():
        o_ref[...] = jnp.dot(pw_ref[0] * ps_ref[0], ow_ref[...], preferred_element_type=F32,
                             precision=lax.Precision.HIGHEST).astype(BF16)

    @pl.when(g >= N_POOL_GROUPS)
    def _():
        o_ref[...] = ow_ref[...].astype(BF16)


def _fold_pool_into_out(pool_w, pool_scale, out_w):
    rows, d = out_w.shape
    last = N_POOL_GROUPS - 1
    return pl.pallas_call(
        _fold_body,
        grid=(rows // POOL_GROUP_DIM,),
        in_specs=[pl.BlockSpec((1, POOL_GROUP_DIM, POOL_GROUP_DIM), lambda g: (jnp.minimum(g, last), 0, 0)),
                  pl.BlockSpec((1, 1, POOL_GROUP_DIM), lambda g: (jnp.minimum(g, last), 0, 0)),
                  pl.BlockSpec((POOL_GROUP_DIM, d), lambda g: (g, 0))],
        out_specs=pl.BlockSpec((POOL_GROUP_DIM, d), lambda g: (g, 0)),
        out_shape=jax.ShapeDtypeStruct((rows, d), BF16),
        name="fold_pool",
    )(pool_w, pool_scale.reshape(N_POOL_GROUPS, 1, POOL_GROUP_DIM), out_w)


def _mixer0(x, mods, row_off, in_w, sg, sb, sgu_w, sgu_bias, out_w, g, b):
    bsz, seq, d = x.shape
    prev, main, nxt = _x_specs(seq)
    band = _pool_bands()
    return pl.pallas_call(
        functools.partial(_mix0_body, seq),
        grid=(bsz, seq // SEQ_TILE),
        in_specs=[prev, main, nxt, _mod_spec(row_off), _mod_spec(row_off), _mod_spec(row_off),
                  _const_spec(in_w.shape),
                  _const_spec(band.shape), _const_spec(sg.shape), _const_spec(sb.shape), _const_spec(sgu_w.shape),
                  _const_spec(sgu_bias.shape), _const_spec(out_w.shape),
                  _const_spec(g.shape), _const_spec(b.shape)],
        out_specs=pl.BlockSpec((1, SEQ_TILE, d), lambda bb, i: (bb, i, 0)),
        out_shape=jax.ShapeDtypeStruct(x.shape, F32),
        compiler_params=pltpu.CompilerParams(vmem_limit_bytes=VMEM_LIMIT_BYTES),
        name="mixer0",
    )(x, x, x, mods[0], mods[1], mods[2], in_w, band, sg, sb, sgu_w, sgu_bias, out_w, g, b)


def _mix1_body(seq, xp_ref, x_ref, xn_ref, sh_ref, sc_ref, gt_ref, pw1_ref, pb1_ref, dww_ref, dwb_ref,
               cg_ref, cb_ref, pw2_ref, pb2_ref, g_ref, b_ref, o_ref, g_s, ge_s, go_s, d_s):
    h_ext = _modulated_ext(xp_ref, x_ref, xn_ref, sh_ref, sc_ref)
    ext = SEQ_TILE + 2 * HALO
    valid = _row_valid(seq)

    for blk in range(CONV_WIDTH // GLU_COLS):
        vcols = slice(blk * GLU_COLS, (blk + 1) * GLU_COLS)
        gcols = slice(CONV_WIDTH + blk * GLU_COLS, CONV_WIDTH + (blk + 1) * GLU_COLS)
        value = _bdot(h_ext, pw1_ref[:, vcols]) + pb1_ref[:, vcols]
        gate = _bdot(h_ext, pw1_ref[:, gcols]) + pb1_ref[:, gcols]
        glu = value * _sigmoid(gate)
        glu = jnp.concatenate([jnp.where(valid[:HALO], glu[:HALO], 0.0), glu[HALO:HALO + SEQ_TILE],
                               jnp.where(valid[HALO + SEQ_TILE:], glu[HALO + SEQ_TILE:], 0.0)], axis=0)
        for i in range(ext // SUBLANES):
            for jj in range(GLU_COLS // LANES):
                j = blk * (GLU_COLS // LANES) + jj
                g_s[pl.ds(i * SUBLANES * LANE_BLOCKS + j, SUBLANES, stride=LANE_BLOCKS), :] = (
                    glu[i * SUBLANES:(i + 1) * SUBLANES, jj * LANES:(jj + 1) * LANES])

    pair_rows = 2 * LANE_BLOCKS
    gt = g_s[...]
    ge_s[...] = gt.reshape(ext // 2, pair_rows, LANES).astype(BF16)
    go_s[pl.ds(0, ext // 2 - 1)] = (
        gt[LANE_BLOCKS:ext * LANE_BLOCKS - LANE_BLOCKS].reshape(ext // 2 - 1, pair_rows, LANES).astype(BF16))

    first = HALO - CONV_TAPS // 2
    n_tiles = CONV_PAIRS + (first + CONV_TAPS - 1) // 2

    def conv_step(i, carry):
        q0 = i * CONV_PAIRS
        te = [ge_s[q0 + n] for n in range(n_tiles)]
        to = [go_s[q0 + n] for n in range(n_tiles)]
        accs = [None] * CONV_PAIRS
        for k in range(CONV_TAPS):
            off = first + k
            wk = dww_ref[k].astype(F32)
            for q in range(CONV_PAIRS):
                tile = (te if off % 2 == 0 else to)[q + off // 2]
                prod = wk * tile.astype(F32)
                accs[q] = prod if accs[q] is None else accs[q] + prod
        for q in range(CONV_PAIRS):
            row = pl.multiple_of((q0 + q) * pair_rows, pair_rows)
            d_s[pl.ds(row, pair_rows), :] = accs[q] + dwb_ref[...]
        return carry

    lax.fori_loop(0, SEQ_TILE // (2 * CONV_PAIRS), conv_step, 0)

    for r0 in range(0, SEQ_TILE, MIX_SUB):
        d = jnp.concatenate(
            [jnp.concatenate([d_s[pl.ds(i * SUBLANES * LANE_BLOCKS + j, SUBLANES, stride=LANE_BLOCKS), :]
                              for j in range(LANE_BLOCKS)], axis=1)
             for i in range(r0 // SUBLANES, (r0 + MIX_SUB) // SUBLANES)], axis=0)
        dn = _layer_norm(d, cg_ref[...], cb_ref[...]).astype(BF16)
        dn = dn * _sigmoid(dn)
        y = _bdot(dn, pw2_ref[...]) + pb2_ref[...]
        r = ALPHA * x_ref[0, pl.ds(r0, MIX_SUB), :] + gt_ref[0] * y
        o_ref[0, pl.ds(r0, MIX_SUB), :] = _layer_norm(r, g_ref[...], b_ref[...])


def _mixer1(x, mods, row_off, pw1_w, pw1_b, dw_w, dw_b, cg, cb, pw2_w, pw2_b, g, b):
    bsz, seq, d = x.shape
    prev, main, nxt = _x_specs(seq)
    return pl.pallas_call(
        functools.partial(_mix1_body, seq),
        grid=(bsz, seq // SEQ_TILE),
        in_specs=[prev, main, nxt, _mod_spec(row_off), _mod_spec(row_off), _mod_spec(row_off),
                  _const_spec(pw1_w.shape), _const_spec(pw1_b.shape), _const_spec(dw_w.shape),
                  _const_spec(dw_b.shape), _const_spec(cg.shape), _const_spec(cb.shape),
                  _const_spec(pw2_w.shape), _const_spec(pw2_b.shape),
                  _const_spec(g.shape), _const_spec(b.shape)],
        out_specs=pl.BlockSpec((1, SEQ_TILE, d), lambda bb, i: (bb, i, 0)),
        out_shape=jax.ShapeDtypeStruct(x.shape, F32),
        scratch_shapes=[pltpu.VMEM(((SEQ_TILE + 2 * HALO) * LANE_BLOCKS, LANES), F32),
                        pltpu.VMEM(((SEQ_TILE + 2 * HALO) // 2, 2 * LANE_BLOCKS, LANES), BF16),
                        pltpu.VMEM(((SEQ_TILE + 2 * HALO) // 2, 2 * LANE_BLOCKS, LANES), BF16),
                        pltpu.VMEM((SEQ_TILE * LANE_BLOCKS, LANES), F32)],
        compiler_params=pltpu.CompilerParams(vmem_limit_bytes=VMEM_LIMIT_BYTES),
        name="mixer1",
    )(x, x, x, mods[0], mods[1], mods[2], pw1_w, pw1_b, dw_w, dw_b, cg, cb, pw2_w, pw2_b, g, b)


def _mlp_body(x_ref, sh_ref, sc_ref, gt_ref, w1_ref, w2_ref, g_ref, b_ref, o_ref):
    for s in range(MLP_TILE // MLP_SUB):
        rows = pl.ds(s * MLP_SUB, MLP_SUB)
        x = x_ref[0, rows, :]
        h = (x * (1.0 + sc_ref[0]) + sh_ref[0]).astype(BF16)
        acc = None
        for f in range(D_FF // FF_CHUNK):
            cols = slice(f * FF_CHUNK, (f + 1) * FF_CHUNK)
            u = jnp.maximum(_bdot(h, w1_ref[:, cols]), 0.0)
            part = _bdot((u * u).astype(BF16), w2_ref[cols, :])
            acc = part if acc is None else acc + part
        r = ALPHA * x + gt_ref[0] * acc
        o_ref[0, rows, :] = _layer_norm(r, g_ref[...], b_ref[...])


def _mlp(x, mods, row_off, w1, w2, g, b):
    bsz, seq, d = x.shape
    tile = pl.BlockSpec((1, MLP_TILE, d), lambda bb, i: (bb, i, 0))
    return pl.pallas_call(
        _mlp_body,
        grid=(bsz, seq // MLP_TILE),
        in_specs=[tile, _mod_spec(row_off), _mod_spec(row_off), _mod_spec(row_off),
                  _const_spec(w1.shape), _const_spec(w2.shape), _const_spec(g.shape), _const_spec(b.shape)],
        out_specs=tile,
        out_shape=jax.ShapeDtypeStruct(x.shape, F32),
        compiler_params=pltpu.CompilerParams(vmem_limit_bytes=VMEM_LIMIT_BYTES),
        name="mlp",
    )(x, mods[3], mods[4], mods[5], w1, w2, g, b)


def _row(v):
    return v.reshape(1, -1)


def kernel(x_prompt, x_sample, c_prompt, c_sample, l0_ada_w, l0_ada_b, l0_in_w, l0_pool_w, l0_pool_scale, l0_sgu_ln_g, l0_sgu_ln_b, l0_sgu_w, l0_sgu_b, l0_out_w, l0_ln1_g, l0_ln1_b, l0_mlp_w1, l0_mlp_w2, l0_ln2_g, l0_ln2_b, l1_ada_w, l1_ada_b, l1_pw1_w, l1_pw1_b, l1_dw_w, l1_dw_b, l1_cnorm_g, l1_cnorm_b, l1_pw2_w, l1_pw2_b, l1_ln1_g, l1_ln1_b, l1_mlp_w1, l1_mlp_w2, l1_ln2_g, l1_ln2_b):
    n_prompt = c_prompt.shape[0]
    c_all = jnp.concatenate([c_prompt, c_sample], axis=0)

    def mods_of(ada_w, ada_b):
        mod = _adaln(c_all, ada_w, ada_b)
        return [m.reshape(-1, 1, D_MODEL) for m in jnp.split(mod, N_MOD, axis=-1)]

    mods0 = mods_of(l0_ada_w, l0_ada_b)
    mods1 = mods_of(l1_ada_w, l1_ada_b)

    sgu_bias = jnp.repeat(l0_sgu_b.T, SGU_HEAD_DIM, axis=1)

    in_w = l0_in_w.astype(BF16)
    sgu_w = l0_sgu_w.astype(BF16)
    out_w = _fold_pool_into_out(l0_pool_w, l0_pool_scale, l0_out_w)
    w1_0, w2_0 = l0_mlp_w1.astype(BF16), l0_mlp_w2.astype(BF16)
    pw1_w, pw2_w = l1_pw1_w.astype(BF16), l1_pw2_w.astype(BF16)
    dw_pair = jnp.tile(l1_dw_w.reshape(CONV_TAPS, 1, LANE_BLOCKS, LANES), (1, 2, 1, 1))
    dw_pair = dw_pair.reshape(CONV_TAPS, 2 * LANE_BLOCKS, LANES).astype(BF16)
    dwb_pair = jnp.tile(l1_dw_b.reshape(LANE_BLOCKS, LANES), (2, 1))
    w1_1, w2_1 = l1_mlp_w1.astype(BF16), l1_mlp_w2.astype(BF16)

    def trunk(x, row_off):
        x = _mixer0(x, mods0, row_off, in_w, _row(l0_sgu_ln_g), _row(l0_sgu_ln_b),
                    sgu_w, sgu_bias, out_w, _row(l0_ln1_g), _row(l0_ln1_b))
        x = _mlp(x, mods0, row_off, w1_0, w2_0, _row(l0_ln2_g), _row(l0_ln2_b))
        x = _mixer1(x, mods1, row_off, pw1_w, _row(l1_pw1_b), dw_pair, dwb_pair, _row(l1_cnorm_g),
                    _row(l1_cnorm_b), pw2_w, _row(l1_pw2_b), _row(l1_ln1_g), _row(l1_ln1_b))
        x = _mlp(x, mods1, row_off, w1_1, w2_1, _row(l1_ln2_g), _row(l1_ln2_b))
        return x

    return (trunk(x_prompt, 0), trunk(x_sample, n_prompt))
```
